```python
import math
import jax, jax.numpy as jnp
from jax import lax
import numpy as np

D_MODEL = 4096
BATCH = 4
SEQ = 4096
DEPTH = 4

D_FF = 6144
MLA_HEADS = 16
Q_LORA = 896
KV_LORA = 512
QK_NOPE = 128
QK_ROPE = 64
V_HEAD = 128
MLA_WIDTH = MLA_HEADS * V_HEAD
ATTN_BLOCK = 128
ROPE_THETA = 10000.0
SSD_HEAD_DIM = 64
SSD_HEADS = 32
SSD_GROUPS = 8
SSD_STATE = 128
SSD_CONV = 4
SSD_CHUNK = 128
SSD_WIDTH = SSD_HEADS * SSD_HEAD_DIM
SSD_CONV_DIM = SSD_WIDTH + 2 * SSD_GROUPS * SSD_STATE
MIX_WIDTH = MLA_WIDTH + SSD_WIDTH
IN_COLS = Q_LORA + KV_LORA + QK_ROPE + SSD_WIDTH + SSD_CONV_DIM + SSD_HEADS
EPS = 1e-6

kernel_name = "hybrid_mla_ssd_macaron_trunk"


def rms_norm(x, g):
    xf = x.astype(jnp.float32)
    y = xf * lax.rsqrt(jnp.mean(xf * xf, axis=-1, keepdims=True) + EPS)
    return (y * g.astype(jnp.float32)).astype(x.dtype)


def swiglu(h, w_gate, w_up, w_down):
    return (jax.nn.silu(h @ w_gate) * (h @ w_up)) @ w_down


def rope_tables(positions):
    inv_freq = ROPE_THETA ** (-jnp.arange(0, QK_ROPE, 2, dtype=jnp.float32) / QK_ROPE)
    ang = positions.astype(jnp.float32)[..., None] * inv_freq
    return jnp.cos(ang), jnp.sin(ang)


def apply_rope(x, cos, sin):
    half = x.shape[-1] // 2
    xf = x.astype(jnp.float32)
    x1, x2 = xf[..., :half], xf[..., half:]
    return jnp.concatenate([x1 * cos - x2 * sin, x2 * cos + x1 * sin], axis=-1).astype(x.dtype)


def mla_group(c_q, c_kv, k_pe, positions, q_a_norm, w_q_up, kv_a_norm, w_kv_up):
    b, s, _ = c_q.shape
    q = (rms_norm(c_q, q_a_norm) @ w_q_up).reshape(b, s, MLA_HEADS, QK_NOPE + QK_ROPE)
    kv = (rms_norm(c_kv, kv_a_norm) @ w_kv_up).reshape(b, s, MLA_HEADS, QK_NOPE + V_HEAD)
    cos, sin = rope_tables(positions)
    q_pe = apply_rope(q[..., QK_NOPE:], cos[:, :, None], sin[:, :, None])
    k_pe = apply_rope(k_pe, cos, sin)
    q = jnp.concatenate([q[..., :QK_NOPE], q_pe], axis=-1)
    k = jnp.concatenate([kv[..., :QK_NOPE], jnp.broadcast_to(k_pe[:, :, None], (b, s, MLA_HEADS, QK_ROPE))], axis=-1)
    v = kv[..., QK_NOPE:]
    scale = (QK_NOPE + QK_ROPE) ** -0.5
    outs = []
    for blk in range(s // ATTN_BLOCK):
        q0, q1 = blk * ATTN_BLOCK, (blk + 1) * ATTN_BLOCK
        scores = jnp.einsum("bqhd,bkhd->bhqk", q[:, q0:q1], k[:, :q1]).astype(jnp.float32) * scale
        causal = jnp.arange(q0, q1)[:, None] >= jnp.arange(q1)[None, :]
        scores = jnp.where(causal, scores, -jnp.inf)
        p = jax.nn.softmax(scores, axis=-1).astype(v.dtype)
        outs.append(jnp.einsum("bhqk,bkhd->bqhd", p, v[:, :q1]))
    return jnp.concatenate(outs, axis=1).reshape(b, s, MLA_WIDTH)


def ssd_chunked_scan(x, dt, A, Bm, Cm):
    b, s, h, p = x.shape
    nc, L = s // SSD_CHUNK, SSD_CHUNK
    G, E = SSD_GROUPS, SSD_HEADS // SSD_GROUPS
    xdt = (x * dt[..., None]).reshape(b, nc, L, G, E, p)
    a = (dt * A).reshape(b, nc, L, G, E).transpose(0, 1, 3, 4, 2)
    Bc = Bm.reshape(b, nc, L, G, SSD_STATE)
    Cc = Cm.reshape(b, nc, L, G, SSD_STATE)
    a_cs = jnp.cumsum(a, axis=-1)
    tril = jnp.tril(jnp.ones((L, L), dtype=bool))
    decay = jnp.exp(jnp.where(tril, a_cs[..., :, None] - a_cs[..., None, :], -jnp.inf))
    cb = jnp.einsum("bclgn,bcsgn->bcgls", Cc, Bc)
    y_diag = jnp.einsum("bcgls,bcgels,bcsgep->bclgep", cb, decay, xdt)
    decay_to_end = jnp.exp(a_cs[..., -1:] - a_cs)
    chunk_states = jnp.einsum("bclgn,bcgel,bclgep->bcgepn", Bc, decay_to_end, xdt)
    chunk_decay = jnp.exp(a_cs[..., -1])

    def step(state, inp):
        dec, new = inp
        return dec[..., None, None] * state + new, state

    init = jnp.zeros((b, G, E, p, SSD_STATE), chunk_states.dtype)
    _, states_in = lax.scan(step, init, (jnp.moveaxis(chunk_decay, 1, 0), jnp.moveaxis(chunk_states, 1, 0)))
    states_in = jnp.moveaxis(states_in, 0, 1)
    y_off = jnp.einsum("bclgn,bcgepn,bcgel->bclgep", Cc, states_in, jnp.exp(a_cs))
    return (y_diag + y_off).reshape(b, s, h, p)


def ssd_group(z, xbc, dt, conv_w, conv_b, dt_bias, a_log, d_skip, ssd_norm):
    b, s, _ = xbc.shape
    xpad = jnp.pad(xbc, ((0, 0), (SSD_CONV - 1, 0), (0, 0)))
    conv = conv_b + sum(conv_w[k] * xpad[:, k:k + s] for k in range(SSD_CONV))
    xbc = jax.nn.silu(conv)
    xs = xbc[..., :SSD_WIDTH].reshape(b, s, SSD_HEADS, SSD_HEAD_DIM)
    Bm = xbc[..., SSD_WIDTH:SSD_WIDTH + SSD_GROUPS * SSD_STATE].reshape(b, s, SSD_GROUPS, SSD_STATE)
    Cm = xbc[..., SSD_WIDTH + SSD_GROUPS * SSD_STATE:].reshape(b, s, SSD_GROUPS, SSD_STATE)
    dt = jax.nn.softplus((dt + dt_bias).astype(jnp.float32))
    A = -jnp.exp(a_log.astype(jnp.float32))
    y = ssd_chunked_scan(xs, dt, A, Bm, Cm)
    y = (y + d_skip[:, None] * xs).reshape(b, s, SSD_WIDTH)
    yg = (y * jax.nn.silu(z)).reshape(b, s, SSD_GROUPS, SSD_WIDTH // SSD_GROUPS).astype(jnp.float32)
    yg = yg * lax.rsqrt(jnp.mean(yg * yg, axis=-1, keepdims=True) + EPS)
    return (yg.reshape(b, s, SSD_WIDTH) * ssd_norm).astype(z.dtype)


def hybrid_mixer(h, positions, w_in, q_a_norm, w_q_up, kv_a_norm, w_kv_up,
                 conv_w, conv_b, dt_bias, a_log, d_skip, ssd_norm, w_out):
    proj = h @ w_in
    o1 = Q_LORA
    o2 = o1 + KV_LORA
    o3 = o2 + QK_ROPE
    o4 = o3 + SSD_WIDTH
    o5 = o4 + SSD_CONV_DIM
    c_q, c_kv, k_pe = proj[..., :o1], proj[..., o1:o2], proj[..., o2:o3]
    z, xbc, dt = proj[..., o3:o4], proj[..., o4:o5], proj[..., o5:]
    y_attn = mla_group(c_q, c_kv, k_pe, positions, q_a_norm, w_q_up, kv_a_norm, w_kv_up)
    y_ssd = ssd_group(z, xbc, dt, conv_w, conv_b, dt_bias, a_log, d_skip, ssd_norm)
    return jnp.concatenate([y_attn, y_ssd], axis=-1) @ w_out


def setup_inputs(seed: int = 0) -> dict:
    key = jax.random.key(seed)
    ks = jax.random.split(key, 26)

    def nrm(k, shape, fan_in):
        return jax.random.normal(k, shape, jnp.float32) * (fan_in ** -0.5)

    def gain(k, shape):
        return 1.0 + 0.02 * jax.random.normal(k, shape, jnp.float32)

    L = DEPTH
    dt0 = jnp.exp(jax.random.uniform(ks[17], (L, SSD_HEADS), jnp.float32)
                  * (math.log(0.1) - math.log(0.001)) + math.log(0.001))
    return {
        "x": jax.random.normal(ks[0], (BATCH, SEQ, D_MODEL), jnp.float32),
        "positions": jnp.broadcast_to(jnp.arange(SEQ, dtype=jnp.int32), (BATCH, SEQ)),
        "ffn1_norm": gain(ks[1], (L, D_MODEL)),
        "ffn1_w_gate": nrm(ks[2], (L, D_MODEL, D_FF), D_MODEL),
        "ffn1_w_up": nrm(ks[3], (L, D_MODEL, D_FF), D_MODEL),
        "ffn1_w_down": nrm(ks[4], (L, D_FF, D_MODEL), D_FF),
        "mix_norm": gain(ks[5], (L, D_MODEL)),
        "w_in": nrm(ks[6], (L, D_MODEL, IN_COLS), D_MODEL),
        "q_a_norm": gain(ks[7], (L, Q_LORA)),
        "w_q_up": nrm(ks[8], (L, Q_LORA, MLA_HEADS * (QK_NOPE + QK_ROPE)), Q_LORA),
        "kv_a_norm": gain(ks[9], (L, KV_LORA)),
        "w_kv_up": nrm(ks[10], (L, KV_LORA, MLA_HEADS * (QK_NOPE + V_HEAD)), KV_LORA),
        "conv_w": nrm(ks[11], (L, SSD_CONV, SSD_CONV_DIM), SSD_CONV),
        "conv_b": 0.02 * jax.random.normal(ks[12], (L, SSD_CONV_DIM), jnp.float32),
        "dt_bias": dt0 + jnp.log(-jnp.expm1(-dt0)),
        "a_log": jnp.log(jax.random.uniform(ks[13], (L, SSD_HEADS), jnp.float32, minval=1.0, maxval=16.0)),
        "d_skip": 1.0 + 0.1 * jax.random.normal(ks[14], (L, SSD_HEADS), jnp.float32),
        "ssd_norm": gain(ks[15], (L, SSD_WIDTH)),
        "w_out": nrm(ks[16], (L, MIX_WIDTH, D_MODEL), MIX_WIDTH),
        "ffn2_norm": gain(ks[18], (L, D_MODEL)),
        "ffn2_w_gate": nrm(ks[19], (L, D_MODEL, D_FF), D_MODEL),
        "ffn2_w_up": nrm(ks[20], (L, D_MODEL, D_FF), D_MODEL),
        "ffn2_w_down": nrm(ks[21], (L, D_FF, D_MODEL), D_FF),
        "final_norm": gain(ks[22], (D_MODEL,)),
    }


def reference(x, positions, ffn1_norm, ffn1_w_gate, ffn1_w_up, ffn1_w_down,
              mix_norm, w_in, q_a_norm, w_q_up, kv_a_norm, w_kv_up,
              conv_w, conv_b, dt_bias, a_log, d_skip, ssd_norm, w_out,
              ffn2_norm, ffn2_w_gate, ffn2_w_up, ffn2_w_down, final_norm):
    for l in range(DEPTH):
        x = x + 0.5 * swiglu(rms_norm(x, ffn1_norm[l]), ffn1_w_gate[l], ffn1_w_up[l], ffn1_w_down[l])
        x = x + hybrid_mixer(rms_norm(x, mix_norm[l]), positions, w_in[l], q_a_norm[l], w_q_up[l],
                             kv_a_norm[l], w_kv_up[l], conv_w[l], conv_b[l], dt_bias[l], a_log[l],
                             d_skip[l], ssd_norm[l], w_out[l])
        x = x + 0.5 * swiglu(rms_norm(x, ffn2_norm[l]), ffn2_w_gate[l], ffn2_w_up[l], ffn2_w_down[l])
    return rms_norm(x, final_norm)
```

```python
import functools
import math

import jax
import jax.numpy as jnp
from jax import lax
from jax.experimental import pallas as pl
from jax.experimental.pallas import tpu as pltpu

F32 = jnp.float32
BF16 = jnp.bfloat16

MLA_HEADS = 16
Q_LORA = 896
KV_LORA = 512
QK_NOPE = 128
QK_ROPE = 64
V_HEAD = 128
ROPE_THETA = 10000.0
SSD_HEAD_DIM = 64
SSD_HEADS = 32
SSD_GROUPS = 8
SSD_STATE = 128
SSD_CONV = 4
SSD_CHUNK = 128
SSD_WIDTH = SSD_HEADS * SSD_HEAD_DIM
MLA_WIDTH = MLA_HEADS * V_HEAD
EPS = 1e-6

LANES = 128
Q_LORA_PAD = 1024
SMALL_COLS = 512
QK_DIM = 256
PROJ_COLS = Q_LORA_PAD + KV_LORA + SMALL_COLS + SSD_WIDTH + (SSD_WIDTH + 2 * SSD_GROUPS * SSD_STATE)

VMEM_LIMIT = 56 * 1024 * 1024


def _params(sem):
    return pltpu.CompilerParams(dimension_semantics=sem, vmem_limit_bytes=VMEM_LIMIT)


def _rms(x, g):
    return x * lax.rsqrt(jnp.mean(x * x, axis=-1, keepdims=True) + EPS) * g


def _silu(x):
    return x * jax.nn.sigmoid(x)


def _ffn_kernel(x_ref, g_ref, wg_ref, wu_ref, wd_ref, *rest, final):
    if final:
        fn_ref, o_ref, h_ref = rest
    else:
        o_ref, h_ref = rest
    f = pl.program_id(1)

    @pl.when(f == 0)
    def _():
        x = x_ref[...]
        h_ref[...] = _rms(x, g_ref[...]).astype(BF16)
        o_ref[...] = x

    h = h_ref[...]
    g = jnp.dot(h, wg_ref[...], preferred_element_type=F32)
    u = jnp.dot(h, wu_ref[...], preferred_element_type=F32)
    a = (0.5 * _silu(g) * u).astype(BF16)
    o_ref[...] += jnp.dot(a, wd_ref[...], preferred_element_type=F32)

    if final:
        @pl.when(f == pl.num_programs(1) - 1)
        def _():
            o_ref[...] = _rms(o_ref[...], fn_ref[...])


def _ffn(x, g, wg, wu, wd, final_norm=None, *, tm=512, tf=256):
    m, d = x.shape
    ff = wg.shape[1]
    tm, tf = min(tm, m), min(tf, ff)
    final = final_norm is not None
    in_specs = [
        pl.BlockSpec((tm, d), lambda i, f: (i, 0), pipeline_mode=pl.Buffered(1)),
        pl.BlockSpec((1, d), lambda i, f: (0, 0)),
        pl.BlockSpec((d, tf), lambda i, f: (0, f)),
        pl.BlockSpec((d, tf), lambda i, f: (0, f)),
        pl.BlockSpec((tf, d), lambda i, f: (f, 0)),
    ]
    args = [x, g.reshape(1, d), wg, wu, wd]
    if final:
        in_specs.append(pl.BlockSpec((1, d), lambda i, f: (0, 0)))
        args.append(final_norm.reshape(1, d))
    return pl.pallas_call(
        functools.partial(_ffn_kernel, final=final),
        grid=(m // tm, ff // tf),
        in_specs=in_specs,
        out_specs=pl.BlockSpec((tm, d), lambda i, f: (i, 0)),
        out_shape=jax.ShapeDtypeStruct((m, d), F32),
        scratch_shapes=[pltpu.VMEM((tm, d), BF16)],
        compiler_params=_params(("parallel", "arbitrary")),
        name="ffn",
    )(*args)


SMALL_BLOCK = (Q_LORA_PAD + KV_LORA) // SMALL_COLS


def _inproj_kernel(x_ref, g_ref, w_ref, o_ref, small_ref, h_ref):
    j = pl.program_id(1)

    @pl.when(j == 0)
    def _():
        h_ref[...] = _rms(x_ref[...], g_ref[...]).astype(BF16)

    y = jnp.dot(h_ref[...], w_ref[...], preferred_element_type=F32)
    o_ref[...] = y.astype(BF16)

    @pl.when(j == SMALL_BLOCK)
    def _():
        small_ref[...] = y


def _inproj(x, g, w, *, tm=1024):
    m, d = x.shape
    n = w.shape[1]
    tm, tn = min(tm, m), SMALL_COLS
    return pl.pallas_call(
        _inproj_kernel,
        grid=(m // tm, n // tn),
        in_specs=[
            pl.BlockSpec((tm, d), lambda i, j: (i, 0), pipeline_mode=pl.Buffered(1)),
            pl.BlockSpec((1, d), lambda i, j: (0, 0)),
            pl.BlockSpec((d, tn), lambda i, j: (0, j)),
        ],
        out_specs=[
            pl.BlockSpec((tm, tn), lambda i, j: (i, j)),
            pl.BlockSpec((tm, tn), lambda i, j: (i, 0)),
        ],
        out_shape=[
            jax.ShapeDtypeStruct((m, n), BF16),
            jax.ShapeDtypeStruct((m, tn), F32),
        ],
        scratch_shapes=[pltpu.VMEM((tm, d), BF16)],
        compiler_params=_params(("parallel", "arbitrary")),
        name="inproj",
    )(x, g.reshape(1, d), w)


def _rope_fold(t):
    return t + pltpu.roll(t, QK_ROPE, axis=1)


def _qkv_kernel(cq_ref, ckv_ref, kpe_ref, pos_ref, freq_ref, qn_ref, kvn_ref, wq_ref, wkv_ref,
                q_ref, k_ref, v_ref):
    hq = _rms(cq_ref[:, :Q_LORA].astype(F32), qn_ref[...]).astype(BF16)
    hkv = _rms(ckv_ref[...].astype(F32), kvn_ref[...]).astype(BF16)
    yq = jnp.dot(hq, wq_ref[...], preferred_element_type=F32)
    ykv = jnp.dot(hkv, wkv_ref[...], preferred_element_type=F32)

    ang = pos_ref[...].astype(F32) * freq_ref[...]
    lane = lax.broadcasted_iota(jnp.int32, ang.shape, 1)
    first_half = lane < QK_ROPE
    table = jnp.where(first_half, jnp.cos(ang), jnp.sin(ang))

    k_rope = jnp.where(first_half, _rope_fold(kpe_ref[...] * table), 0.0).astype(BF16)
    nope_w = MLA_HEADS * QK_NOPE
    for h in range(MLA_HEADS):
        lo = h * LANES
        q_ref[0, h, :, :QK_NOPE] = yq[:, lo:lo + QK_NOPE].astype(BF16)
        q_ref[0, h, :, QK_NOPE:] = _rope_fold(yq[:, nope_w + lo:nope_w + lo + LANES] * table).astype(BF16)
        k_ref[0, h, :, :QK_NOPE] = ykv[:, lo:lo + QK_NOPE].astype(BF16)
        k_ref[0, h, :, QK_NOPE:] = k_rope
        v_ref[0, h] = ykv[:, nope_w + lo:nope_w + lo + V_HEAD].astype(BF16)


def _qkv(proj, small, pos, freq, qn, kvn, wq, wkv, *, batch, seq, tm=512):
    tm = min(tm, seq)
    nt = seq // tm
    row = lambda b, s: b * nt + s
    kv_blk = Q_LORA_PAD // KV_LORA
    hd = MLA_HEADS
    return pl.pallas_call(
        _qkv_kernel,
        grid=(batch, nt),
        in_specs=[
            pl.BlockSpec((tm, Q_LORA_PAD), lambda b, s: (row(b, s), 0)),
            pl.BlockSpec((tm, KV_LORA), lambda b, s: (row(b, s), kv_blk)),
            pl.BlockSpec((tm, LANES), lambda b, s: (row(b, s), 0)),
            pl.BlockSpec((tm, 1), lambda b, s: (row(b, s), 0)),
            pl.BlockSpec((1, LANES), lambda b, s: (0, 0)),
            pl.BlockSpec((1, Q_LORA), lambda b, s: (0, 0)),
            pl.BlockSpec((1, KV_LORA), lambda b, s: (0, 0)),
            pl.BlockSpec(wq.shape, lambda b, s: (0, 0)),
            pl.BlockSpec(wkv.shape, lambda b, s: (0, 0)),
        ],
        out_specs=[
            pl.BlockSpec((1, hd, tm, QK_DIM), lambda b, s: (b, 0, s, 0)),
            pl.BlockSpec((1, hd, tm, QK_DIM), lambda b, s: (b, 0, s, 0)),
            pl.BlockSpec((1, hd, tm, V_HEAD), lambda b, s: (b, 0, s, 0)),
        ],
        out_shape=[
            jax.ShapeDtypeStruct((batch, hd, seq, QK_DIM), BF16),
            jax.ShapeDtypeStruct((batch, hd, seq, QK_DIM), BF16),
            jax.ShapeDtypeStruct((batch, hd, seq, V_HEAD), BF16),
        ],
        compiler_params=_params(("parallel", "parallel")),
        name="qkv_up",
    )(proj, proj, small, pos, freq, qn.reshape(1, -1), kvn.reshape(1, -1), wq, wkv)


def _attn_kernel(q_ref, k_ref, v_ref, o_ref, m_ref, l_ref, acc_ref, *, tq, scale):
    qi = pl.program_id(2)
    q = q_ref[0, 0]
    m_ref[...] = jnp.full(m_ref.shape, -jnp.inf, F32)
    l_ref[...] = jnp.zeros(l_ref.shape, F32)
    acc_ref[...] = jnp.zeros(acc_ref.shape, F32)

    def step(j, masked):
        start = pl.multiple_of(j * tq, tq)
        k = k_ref[0, 0, pl.ds(start, tq), :]
        v = v_ref[0, 0, pl.ds(start, tq), :]
        s = lax.dot_general(q, k, (((1,), (1,)), ((), ())), preferred_element_type=F32) * scale
        if masked:
            r = lax.broadcasted_iota(jnp.int32, s.shape, 0)
            c = lax.broadcasted_iota(jnp.int32, s.shape, 1)
            s = jnp.where(r >= c, s, -jnp.inf)
        m_old = m_ref[...]
        m_new = jnp.maximum(m_old, jnp.max(s, axis=-1, keepdims=True))
        alpha = jnp.exp(m_old - m_new)
        p = jnp.exp(s - m_new)
        l_ref[...] = alpha * l_ref[...] + jnp.sum(p, axis=-1, keepdims=True)
        acc_ref[...] = alpha * acc_ref[...] + jnp.dot(p.astype(BF16), v, preferred_element_type=F32)
        m_ref[...] = m_new

    step(qi, True)

    def body(j, carry):
        step(j, False)
        return carry

    lax.fori_loop(0, qi, body, 0)
    o_ref[0] = (acc_ref[...] / l_ref[...]).astype(o_ref.dtype)


def _attention(q, k, v, *, tq=512):
    b, hd, s, _ = q.shape
    tq = min(tq, s)
    scale = (QK_NOPE + QK_ROPE) ** -0.5
    return pl.pallas_call(
        functools.partial(_attn_kernel, tq=tq, scale=scale),
        grid=(b, hd, s // tq),
        in_specs=[
            pl.BlockSpec((1, 1, tq, QK_DIM), lambda bi, h, i: (bi, h, i, 0)),
            pl.BlockSpec((1, 1, s, QK_DIM), lambda bi, h, i: (bi, h, 0, 0)),
            pl.BlockSpec((1, 1, s, V_HEAD), lambda bi, h, i: (bi, h, 0, 0)),
        ],
        out_specs=pl.BlockSpec((1, tq, V_HEAD), lambda bi, h, i: (bi, i, h)),
        out_shape=jax.ShapeDtypeStruct((b, s, hd * V_HEAD), BF16),
        scratch_shapes=[
            pltpu.VMEM((tq, 1), F32),
            pltpu.VMEM((tq, 1), F32),
            pltpu.VMEM((tq, V_HEAD), F32),
        ],
        compiler_params=_params(("parallel", "parallel", "arbitrary")),
        name="mla_attention",
    )(q, k, v)


CONV_HALO = 8
HEADS_PER_GROUP = SSD_HEADS // SSD_GROUPS
GROUP_WIDTH = HEADS_PER_GROUP * SSD_HEAD_DIM


def _ssd_kernel(xbc_ref, z_ref, dt_ref, cw_ref, cb_ref, dtb_ref, alog_ref, dskip_ref, nw_ref,
                o_ref, ext_ref, st_ref):
    L = SSD_CHUNK
    c = pl.program_id(1)

    @pl.when(c == 0)
    def _():
        ext_ref[0:CONV_HALO, :] = jnp.zeros((CONV_HALO, ext_ref.shape[1]), F32)
        st_ref[...] = jnp.zeros(st_ref.shape, F32)

    @pl.when(c > 0)
    def _():
        ext_ref[0:CONV_HALO, :] = ext_ref[L:L + CONV_HALO, :]

    ext_ref[CONV_HALO:CONV_HALO + L, :] = xbc_ref[...].astype(F32)
    conv = cb_ref[...]
    for t in range(SSD_CONV):
        off = CONV_HALO - (SSD_CONV - 1) + t
        conv = conv + cw_ref[t:t + 1, :] * ext_ref[off:off + L, :]
    u = _silu(conv)
    b_off = SSD_WIDTH
    c_off = SSD_WIDTH + SSD_GROUPS * SSD_STATE

    dt = jax.nn.softplus(dt_ref[...] + dtb_ref[...])
    a = dt * (-jnp.exp(alog_ref[...]))
    row = lax.broadcasted_iota(jnp.int32, (L, L), 0)
    col = lax.broadcasted_iota(jnp.int32, (L, L), 1)
    tril = row >= col
    a_cs = jnp.dot(tril.astype(F32), a, preferred_element_type=F32,
                   precision=lax.Precision.HIGHEST)
    a_cs_t = jnp.dot(a.T, (row <= col).astype(F32), preferred_element_type=F32,
                     precision=lax.Precision.HIGHEST)
    a_end = a_cs[L - 1:L, :]
    e_cs = jnp.exp(a_cs)
    e_end = jnp.exp(a_end - a_cs)
    lane = lax.broadcasted_iota(jnp.int32, (L, LANES), 1)
    lo_half = lane < SSD_HEAD_DIM
    lane1 = lax.broadcasted_iota(jnp.int32, (1, LANES), 1)

    def pair_cols(v, h0):
        return jnp.where(lo_half, v[:, h0:h0 + 1], v[:, h0 + 1:h0 + 2])

    for g in range(SSD_GROUPS):
        bg = u[:, b_off + g * SSD_STATE:b_off + (g + 1) * SSD_STATE].astype(BF16)
        cg = u[:, c_off + g * SSD_STATE:c_off + (g + 1) * SSD_STATE].astype(BF16)
        cb = lax.dot_general(cg, bg, (((1,), (1,)), ((), ())), preferred_element_type=F32)
        st = st_ref[g]
        y_off = jnp.dot(cg, st.astype(BF16), preferred_element_type=F32)
        ys, xscs, cds = [], [], []
        for pr in range(HEADS_PER_GROUP // 2):
            h0 = g * HEADS_PER_GROUP + 2 * pr
            x0 = h0 * SSD_HEAD_DIM
            xp = u[:, x0:x0 + LANES]
            xdt = xp * pair_cols(dt, h0)
            ms = []
            for h in (h0, h0 + 1):
                diff = a_cs[:, h:h + 1] - a_cs_t[h:h + 1, :]
                ms.append((cb * jnp.exp(jnp.where(tril, diff, -jnp.inf))).astype(BF16))
            m2 = jnp.concatenate(ms, axis=1)
            xd = jnp.concatenate([jnp.where(lo_half, xdt, 0.0), jnp.where(lo_half, 0.0, xdt)],
                                 axis=0).astype(BF16)
            y_diag = jnp.dot(m2, xd, preferred_element_type=F32)
            yo = y_off[:, 2 * pr * SSD_HEAD_DIM:2 * pr * SSD_HEAD_DIM + LANES]
            ys.append(y_diag + pair_cols(e_cs, h0) * yo + dskip_ref[:, x0:x0 + LANES] * xp)
            xscs.append((xdt * pair_cols(e_end, h0)).astype(BF16))
            cds.append(jnp.where(lane1 < SSD_HEAD_DIM, a_end[:, h0:h0 + 1], a_end[:, h0 + 1:h0 + 2]))
        xsc = jnp.concatenate(xscs, axis=1)
        new = lax.dot_general(bg, xsc, (((0,), (0,)), ((), ())), preferred_element_type=F32)
        st_ref[g] = st * jnp.exp(jnp.concatenate(cds, axis=1)) + new

        gw = slice(g * GROUP_WIDTH, (g + 1) * GROUP_WIDTH)
        yg = jnp.concatenate(ys, axis=1) * _silu(z_ref[:, gw].astype(F32))
        yg = yg * lax.rsqrt(jnp.mean(yg * yg, axis=-1, keepdims=True) + EPS)
        o_ref[:, gw] = (yg * nw_ref[:, gw]).astype(o_ref.dtype)


def _ssd(proj, small, cw, cb, dtb, alog, dskip, nw, *, batch, seq):
    L = SSD_CHUNK
    nc = seq // L
    conv_dim = cw.shape[1]
    row = lambda b, c: b * nc + c
    xbc_blk = (PROJ_COLS - conv_dim) // conv_dim
    z_blk = (PROJ_COLS - conv_dim - SSD_WIDTH) // SSD_WIDTH
    dt_blk = 1
    const = lambda b, c: (0, 0)
    return pl.pallas_call(
        _ssd_kernel,
        grid=(batch, nc),
        in_specs=[
            pl.BlockSpec((L, conv_dim), lambda b, c: (row(b, c), xbc_blk)),
            pl.BlockSpec((L, SSD_WIDTH), lambda b, c: (row(b, c), z_blk)),
            pl.BlockSpec((L, LANES), lambda b, c: (row(b, c), dt_blk)),
            pl.BlockSpec((SSD_CONV, conv_dim), const),
            pl.BlockSpec((1, conv_dim), const),
            pl.BlockSpec((1, LANES), const),
            pl.BlockSpec((1, LANES), const),
            pl.BlockSpec((1, SSD_WIDTH), const),
            pl.BlockSpec((1, SSD_WIDTH), const),
        ],
        out_specs=pl.BlockSpec((L, SSD_WIDTH), lambda b, c: (row(b, c), 0)),
        out_shape=jax.ShapeDtypeStruct((batch * seq, SSD_WIDTH), BF16),
        scratch_shapes=[
            pltpu.VMEM((CONV_HALO + L, conv_dim), F32),
            pltpu.VMEM((SSD_GROUPS, SSD_STATE, GROUP_WIDTH), F32),
        ],
        compiler_params=_params(("parallel", "arbitrary")),
        name="ssd",
    )(proj, proj, small, cw, cb, dtb, alog, dskip, nw)


def _outproj_kernel(ya_ref, ys_ref, wa_ref, ws_ref, x_ref, o_ref):
    o_ref[...] = (x_ref[...]
                  + jnp.dot(ya_ref[...], wa_ref[...], preferred_element_type=F32)
                  + jnp.dot(ys_ref[...], ws_ref[...], preferred_element_type=F32))


def _outproj(ya, ys, w, x, *, tm=1024, tn=512):
    m, d = x.shape
    ka, ks = ya.shape[1], ys.shape[1]
    assert ka == ks
    tm, tn = min(tm, m), min(tn, d)
    return pl.pallas_call(
        _outproj_kernel,
        grid=(m // tm, d // tn),
        in_specs=[
            pl.BlockSpec((tm, ka), lambda i, j: (i, 0)),
            pl.BlockSpec((tm, ks), lambda i, j: (i, 0)),
            pl.BlockSpec((ka, tn), lambda i, j: (0, j)),
            pl.BlockSpec((ks, tn), lambda i, j: (1, j)),
            pl.BlockSpec((tm, tn), lambda i, j: (i, j)),
        ],
        out_specs=pl.BlockSpec((tm, tn), lambda i, j: (i, j)),
        out_shape=jax.ShapeDtypeStruct((m, d), F32),
        compiler_params=_params(("parallel", "arbitrary")),
        name="outproj",
    )(ya, ys, w, w, x)


def _rotate_half_cols(w):
    half = w.shape[-1] // 2
    return jnp.concatenate([-w[..., half:], w[..., :half]], axis=-1)


def _prep_w_in(w_in):
    d = w_in.shape[0]
    o1 = Q_LORA
    o2 = o1 + KV_LORA
    o3 = o2 + QK_ROPE
    o4 = o3 + SSD_WIDTH
    o5 = w_in.shape[1] - SSD_HEADS
    kpe = w_in[:, o2:o3]
    small = jnp.concatenate(
        [kpe, _rotate_half_cols(kpe), w_in[:, o5:],
         jnp.zeros((d, SMALL_COLS - 2 * QK_ROPE - SSD_HEADS), w_in.dtype)], axis=1)
    return jnp.concatenate(
        [w_in[:, :o1], jnp.zeros((d, Q_LORA_PAD - Q_LORA), w_in.dtype), w_in[:, o1:o2], small,
         w_in[:, o3:o4], w_in[:, o4:o5]], axis=1).astype(BF16)


def _prep_w_q(w):
    r = w.shape[0]
    w = w.reshape(r, MLA_HEADS, QK_NOPE + QK_ROPE)
    rope = w[..., QK_NOPE:]
    rope = jnp.concatenate([rope, _rotate_half_cols(rope)], axis=-1)
    return jnp.concatenate([w[..., :QK_NOPE].reshape(r, -1), rope.reshape(r, -1)], axis=1).astype(BF16)


def _prep_w_kv(w):
    r = w.shape[0]
    w = w.reshape(r, MLA_HEADS, QK_NOPE + V_HEAD)
    return jnp.concatenate([w[..., :QK_NOPE].reshape(r, -1), w[..., QK_NOPE:].reshape(r, -1)],
                           axis=1).astype(BF16)


def _pad_lanes(v):
    return jnp.pad(v, (0, LANES - v.shape[0])).reshape(1, LANES)


def kernel(x, positions, ffn1_norm, ffn1_w_gate, ffn1_w_up, ffn1_w_down, mix_norm, w_in, q_a_norm, w_q_up, kv_a_norm, w_kv_up, conv_w, conv_b, dt_bias, a_log, d_skip, ssd_norm, w_out, ffn2_norm, ffn2_w_gate, ffn2_w_up, ffn2_w_down, final_norm):
    batch, seq, d = x.shape
    depth = w_in.shape[0]
    m = batch * seq
    xs = x.reshape(m, d)
    pos = positions.reshape(m, 1)
    inv_freq = ROPE_THETA ** (-jnp.arange(0, QK_ROPE, 2, dtype=F32) / QK_ROPE)
    freq = jnp.tile(inv_freq, LANES // inv_freq.shape[0]).reshape(1, LANES)

    for l in range(depth):
        xs = _ffn(xs, ffn1_norm[l], ffn1_w_gate[l].astype(BF16), ffn1_w_up[l].astype(BF16),
                  ffn1_w_down[l].astype(BF16))
        proj, small = _inproj(xs, mix_norm[l], _prep_w_in(w_in[l]))
        q, k, v = _qkv(proj, small, pos, freq, q_a_norm[l], kv_a_norm[l], _prep_w_q(w_q_up[l]),
                       _prep_w_kv(w_kv_up[l]), batch=batch, seq=seq)
        y_attn = _attention(q, k, v).reshape(m, MLA_WIDTH)
        y_ssd = _ssd(proj, small, conv_w[l], conv_b[l].reshape(1, -1), _pad_lanes(dt_bias[l]),
                     _pad_lanes(a_log[l]), jnp.repeat(d_skip[l], SSD_HEAD_DIM).reshape(1, -1),
                     ssd_norm[l].reshape(1, -1), batch=batch, seq=seq)
        xs = _outproj(y_attn, y_ssd, w_out[l].astype(BF16), xs)
        xs = _ffn(xs, ffn2_norm[l], ffn2_w_gate[l].astype(BF16), ffn2_w_up[l].astype(BF16),
                  ffn2_w_down[l].astype(BF16), final_norm if l == depth - 1 else None)
    return xs.reshape(batch, seq, d)
```

```python
import functools
import math

import jax
import jax.numpy as jnp
from jax import lax
from jax.experimental import pallas as pl
from jax.experimental.pallas import tpu as pltpu

F32 = jnp.float32
BF16 = jnp.bfloat16

MLA_HEADS = 16
Q_LORA = 896
KV_LORA = 512
QK_NOPE = 128
QK_ROPE = 64
V_HEAD = 128
ROPE_THETA = 10000.0
SSD_HEAD_DIM = 64
SSD_HEADS = 32
SSD_GROUPS = 8
SSD_STATE = 128
SSD_CONV = 4
SSD_CHUNK = 128
SSD_WIDTH = SSD_HEADS * SSD_HEAD_DIM
MLA_WIDTH = MLA_HEADS * V_HEAD
EPS = 1e-6

LANES = 128
Q_LORA_PAD = 1024
SMALL_COLS = 512
QK_DIM = 256
PROJ_COLS = Q_LORA_PAD + KV_LORA + SMALL_COLS + SSD_WIDTH + (SSD_WIDTH + 2 * SSD_GROUPS * SSD_STATE)

ATTN_TILE = 512

DOWN_CHUNK = 1024

NORM_ROWS = 128

VMEM_LIMIT = 60 * 1024 * 1024


def _params(sem):
    return pltpu.CompilerParams(dimension_semantics=sem, vmem_limit_bytes=VMEM_LIMIT)


def _rms(x, g):
    return x * lax.rsqrt(jnp.mean(x * x, axis=-1, keepdims=True) + EPS) * g


def _row_chunks(rows):
    step = min(rows, NORM_ROWS)
    return [slice(r, r + step) for r in range(0, rows, step)]


def _silu(x):
    return x * jax.nn.sigmoid(x)


def _ffn_kernel(x_ref, g_ref, wg_ref, wu_ref, wd_ref, *rest, final):
    if final:
        fn_ref, o_ref, h_ref = rest
    else:
        o_ref, h_ref = rest
    f = pl.program_id(1)

    @pl.when(f == 0)
    def _():
        for rows in _row_chunks(x_ref.shape[0]):
            x = x_ref[rows, :]
            h_ref[rows, :] = _rms(x, g_ref[...]).astype(BF16)
            o_ref[rows, :] = x

    h = h_ref[...]
    g = jnp.dot(h, wg_ref[...], preferred_element_type=F32)
    u = jnp.dot(h, wu_ref[...], preferred_element_type=F32)
    a = (0.5 * _silu(g) * u).astype(BF16)
    for c0 in range(0, o_ref.shape[1], DOWN_CHUNK):
        cols = slice(c0, c0 + DOWN_CHUNK)
        o_ref[:, cols] += jnp.dot(a, wd_ref[:, cols], preferred_element_type=F32)

    if final:
        @pl.when(f == pl.num_programs(1) - 1)
        def _():
            for rows in _row_chunks(o_ref.shape[0]):
                o_ref[rows, :] = _rms(o_ref[rows, :], fn_ref[...])


def _ffn(x, g, wg, wu, wd, layer, final_norm=None, *, tm=512, tf=512):
    m, d = x.shape
    ff = wg.shape[2]
    tm, tf = min(tm, m), min(tf, ff)
    final = final_norm is not None
    in_specs = [
        pl.BlockSpec((tm, d), lambda i, f: (i, 0), pipeline_mode=pl.Buffered(1)),
        pl.BlockSpec((1, d), lambda i, f: (0, 0)),
        pl.BlockSpec((None, d, tf), lambda i, f: (layer, 0, f)),
        pl.BlockSpec((None, d, tf), lambda i, f: (layer, 0, f)),
        pl.BlockSpec((None, tf, d), lambda i, f: (layer, f, 0)),
    ]
    args = [x, g.reshape(1, d), wg, wu, wd]
    if final:
        in_specs.append(pl.BlockSpec((1, d), lambda i, f: (0, 0)))
        args.append(final_norm.reshape(1, d))
    return pl.pallas_call(
        functools.partial(_ffn_kernel, final=final),
        grid=(m // tm, ff // tf),
        in_specs=in_specs,
        out_specs=pl.BlockSpec((tm, d), lambda i, f: (i, 0)),
        out_shape=jax.ShapeDtypeStruct((m, d), F32),
        scratch_shapes=[pltpu.VMEM((tm, d), BF16)],
        compiler_params=_params(("parallel", "arbitrary")),
        name="ffn",
    )(*args)


SMALL_BLOCK = (Q_LORA_PAD + KV_LORA) // SMALL_COLS


def _inproj_kernel(x_ref, g_ref, w_ref, o_ref, small_ref, h_ref):
    j = pl.program_id(1)

    @pl.when(j == 0)
    def _():
        for rows in _row_chunks(x_ref.shape[0]):
            h_ref[rows, :] = _rms(x_ref[rows, :], g_ref[...]).astype(BF16)

    y = jnp.dot(h_ref[...], w_ref[...], preferred_element_type=F32)
    o_ref[...] = y.astype(BF16)

    @pl.when(j == SMALL_BLOCK)
    def _():
        small_ref[...] = y


def _inproj(x, g, w, layer, *, tm=1024):
    m, d = x.shape
    n = w.shape[2]
    tm, tn = min(tm, m), SMALL_COLS
    return pl.pallas_call(
        _inproj_kernel,
        grid=(m // tm, n // tn),
        in_specs=[
            pl.BlockSpec((tm, d), lambda i, j: (i, 0), pipeline_mode=pl.Buffered(1)),
            pl.BlockSpec((1, d), lambda i, j: (0, 0)),
            pl.BlockSpec((None, d, tn), lambda i, j: (layer, 0, j)),
        ],
        out_specs=[
            pl.BlockSpec((tm, tn), lambda i, j: (i, j)),
            pl.BlockSpec((tm, tn), lambda i, j: (i, 0)),
        ],
        out_shape=[
            jax.ShapeDtypeStruct((m, n), BF16),
            jax.ShapeDtypeStruct((m, tn), F32),
        ],
        scratch_shapes=[pltpu.VMEM((tm, d), BF16)],
        compiler_params=_params(("parallel", "arbitrary")),
        name="inproj",
    )(x, g.reshape(1, d), w)


def _rope_fold(t):
    return t + pltpu.roll(t, QK_ROPE, axis=1)


def _qkv_kernel(cq_ref, ckv_ref, kpe_ref, pos_ref, freq_ref, qn_ref, kvn_ref, wq_ref, wkv_ref,
                q_ref, k_ref, vt_ref):
    hq = _rms(cq_ref[:, :Q_LORA].astype(F32), qn_ref[...]).astype(BF16)
    hkv = _rms(ckv_ref[...].astype(F32), kvn_ref[...]).astype(BF16)
    yq = jnp.dot(hq, wq_ref[...], preferred_element_type=F32)
    ykv = jnp.dot(hkv, wkv_ref[...], preferred_element_type=F32)

    ang = pos_ref[...].astype(F32) * freq_ref[...]
    lane = lax.broadcasted_iota(jnp.int32, ang.shape, 1)
    first_half = lane < QK_ROPE
    table = jnp.where(first_half, jnp.cos(ang), jnp.sin(ang))

    k_rope = jnp.where(first_half, _rope_fold(kpe_ref[...] * table), 0.0).astype(BF16)
    nope_w = MLA_HEADS * QK_NOPE
    for h in range(MLA_HEADS):
        lo = h * LANES
        q_ref[0, h, :, :QK_NOPE] = yq[:, lo:lo + QK_NOPE].astype(BF16)
        q_ref[0, h, :, QK_NOPE:] = _rope_fold(yq[:, nope_w + lo:nope_w + lo + LANES] * table).astype(BF16)
        k_ref[0, h, :, :QK_NOPE] = ykv[:, lo:lo + QK_NOPE].astype(BF16)
        k_ref[0, h, :, QK_NOPE:] = k_rope
        vt_ref[0, h, 0] = ykv[:, nope_w + lo:nope_w + lo + V_HEAD].T.astype(BF16)


def _qkv(proj, small, pos, freq, qn, kvn, wq, wkv, layer, *, batch, seq):
    tm = min(ATTN_TILE, seq)
    nt = seq // tm
    row = lambda b, s: b * nt + s
    kv_blk = Q_LORA_PAD // KV_LORA
    hd = MLA_HEADS
    return pl.pallas_call(
        _qkv_kernel,
        grid=(batch, nt),
        in_specs=[
            pl.BlockSpec((tm, Q_LORA_PAD), lambda b, s: (row(b, s), 0)),
            pl.BlockSpec((tm, KV_LORA), lambda b, s: (row(b, s), kv_blk)),
            pl.BlockSpec((tm, LANES), lambda b, s: (row(b, s), 0)),
            pl.BlockSpec((tm, 1), lambda b, s: (row(b, s), 0)),
            pl.BlockSpec((1, LANES), lambda b, s: (0, 0)),
            pl.BlockSpec((1, Q_LORA), lambda b, s: (0, 0)),
            pl.BlockSpec((1, KV_LORA), lambda b, s: (0, 0)),
            pl.BlockSpec((None,) + wq.shape[1:], lambda b, s: (layer, 0, 0)),
            pl.BlockSpec((None,) + wkv.shape[1:], lambda b, s: (layer, 0, 0)),
        ],
        out_specs=[
            pl.BlockSpec((1, hd, tm, QK_DIM), lambda b, s: (b, 0, s, 0)),
            pl.BlockSpec((1, hd, tm, QK_DIM), lambda b, s: (b, 0, s, 0)),
            pl.BlockSpec((1, hd, 1, V_HEAD, tm), lambda b, s: (b, 0, s, 0, 0)),
        ],
        out_shape=[
            jax.ShapeDtypeStruct((batch, hd, seq, QK_DIM), BF16),
            jax.ShapeDtypeStruct((batch, hd, seq, QK_DIM), BF16),
            jax.ShapeDtypeStruct((batch, hd, nt, V_HEAD, tm), BF16),
        ],
        compiler_params=_params(("parallel", "parallel")),
        name="qkv_up",
    )(proj, proj, small, pos, freq, qn.reshape(1, -1), kvn.reshape(1, -1), wq, wkv)


def _attn_kernel(q_ref, k_ref, vt_ref, o_ref, s0_ref, s1_ref, m_ref, l_ref, acc_ref, *, tq,
                 exp2_scale):
    qi = pl.program_id(2)
    q = q_ref[0, 0]

    def scores_into(s_ref, j):
        start = pl.multiple_of(j * tq, tq)
        k = k_ref[0, 0, pl.ds(start, tq), :]
        s_ref[...] = lax.dot_general(k, q, (((1,), (1,)), ((), ())),
                                     preferred_element_type=F32) * exp2_scale

    def update(s, j):
        m_old = m_ref[...]
        m_new = jnp.maximum(m_old, jnp.max(s, axis=0, keepdims=True))
        alpha = jnp.exp2(m_old - m_new)
        p = jnp.exp2(s - m_new)
        l_ref[...] = alpha * l_ref[...] + jnp.sum(p, axis=0, keepdims=True)
        acc_ref[...] = alpha * acc_ref[...] + jnp.dot(vt_ref[0, 0, j], p.astype(BF16),
                                                      preferred_element_type=F32)
        m_ref[...] = m_new

    def diagonal(s_ref):
        s = s_ref[...]
        key = lax.broadcasted_iota(jnp.int32, s.shape, 0)
        query = lax.broadcasted_iota(jnp.int32, s.shape, 1)
        update(jnp.where(query >= key, s, -jnp.inf), qi)

    m_ref[...] = jnp.full(m_ref.shape, -jnp.inf, F32)
    l_ref[...] = jnp.zeros(l_ref.shape, F32)
    acc_ref[...] = jnp.zeros(acc_ref.shape, F32)
    scores_into(s0_ref, 0)

    def pair(jj, carry):
        j = 2 * jj
        scores_into(s1_ref, j + 1)
        update(s0_ref[...], j)
        scores_into(s0_ref, j + 2)
        update(s1_ref[...], j + 1)
        return carry

    lax.fori_loop(0, qi // 2, pair, 0)

    @pl.when(qi % 2 == 1)
    def _():
        scores_into(s1_ref, qi)
        update(s0_ref[...], qi - 1)
        diagonal(s1_ref)

    @pl.when(qi % 2 == 0)
    def _():
        diagonal(s0_ref)

    o_ref[0] = (acc_ref[...] / l_ref[...]).T.astype(o_ref.dtype)


def _attention(q, k, vt):
    b, hd, s, _ = q.shape
    nt, tq = vt.shape[2], vt.shape[4]
    exp2_scale = (QK_NOPE + QK_ROPE) ** -0.5 * math.log2(math.e)
    return pl.pallas_call(
        functools.partial(_attn_kernel, tq=tq, exp2_scale=exp2_scale),
        grid=(b, hd, nt),
        in_specs=[
            pl.BlockSpec((1, 1, tq, QK_DIM), lambda bi, h, i: (bi, h, i, 0)),
            pl.BlockSpec((1, 1, s, QK_DIM), lambda bi, h, i: (bi, h, 0, 0)),
            pl.BlockSpec((1, 1, nt, V_HEAD, tq), lambda bi, h, i: (bi, h, 0, 0, 0)),
        ],
        out_specs=pl.BlockSpec((1, tq, V_HEAD), lambda bi, h, i: (bi, i, h)),
        out_shape=jax.ShapeDtypeStruct((b, s, hd * V_HEAD), BF16),
        scratch_shapes=[
            pltpu.VMEM((tq, tq), F32),
            pltpu.VMEM((tq, tq), F32),
            pltpu.VMEM((1, tq), F32),
            pltpu.VMEM((1, tq), F32),
            pltpu.VMEM((V_HEAD, tq), F32),
        ],
        compiler_params=_params(("parallel", "parallel", "arbitrary")),
        name="mla_attention",
    )(q, k, vt)


CONV_HALO = 8
HEADS_PER_GROUP = SSD_HEADS // SSD_GROUPS
GROUP_WIDTH = HEADS_PER_GROUP * SSD_HEAD_DIM


def _ssd_kernel(xbc_ref, z_ref, dt_ref, cw_ref, cb_ref, dtb_ref, alog_ref, dskip_ref, nw_ref,
                o_ref, ext_ref, st_ref):
    L = SSD_CHUNK
    c = pl.program_id(1)

    @pl.when(c == 0)
    def _():
        ext_ref[0:CONV_HALO, :] = jnp.zeros((CONV_HALO, ext_ref.shape[1]), F32)
        st_ref[...] = jnp.zeros(st_ref.shape, F32)

    @pl.when(c > 0)
    def _():
        ext_ref[0:CONV_HALO, :] = ext_ref[L:L + CONV_HALO, :]

    ext_ref[CONV_HALO:CONV_HALO + L, :] = xbc_ref[...].astype(F32)
    conv = cb_ref[...]
    for t in range(SSD_CONV):
        off = CONV_HALO - (SSD_CONV - 1) + t
        conv = conv + cw_ref[t:t + 1, :] * ext_ref[off:off + L, :]
    u = _silu(conv)
    b_off = SSD_WIDTH
    c_off = SSD_WIDTH + SSD_GROUPS * SSD_STATE

    dt = jax.nn.softplus(dt_ref[...] + dtb_ref[...])
    a = dt * (-jnp.exp(alog_ref[...]))
    row = lax.broadcasted_iota(jnp.int32, (L, L), 0)
    col = lax.broadcasted_iota(jnp.int32, (L, L), 1)
    tril = row >= col
    a_cs = jnp.dot(tril.astype(F32), a, preferred_element_type=F32,
                   precision=lax.Precision.HIGHEST)
    a_cs_t = jnp.dot(a.T, (row <= col).astype(F32), preferred_element_type=F32,
                     precision=lax.Precision.HIGHEST)
    a_end = a_cs[L - 1:L, :]
    e_cs = jnp.exp(a_cs)
    e_end = jnp.exp(a_end - a_cs)
    lane = lax.broadcasted_iota(jnp.int32, (L, LANES), 1)
    lo_half = lane < SSD_HEAD_DIM
    lane1 = lax.broadcasted_iota(jnp.int32, (1, LANES), 1)

    def pair_cols(v, h0):
        return jnp.where(lo_half, v[:, h0:h0 + 1], v[:, h0 + 1:h0 + 2])

    for g in range(SSD_GROUPS):
        bg = u[:, b_off + g * SSD_STATE:b_off + (g + 1) * SSD_STATE].astype(BF16)
        cg = u[:, c_off + g * SSD_STATE:c_off + (g + 1) * SSD_STATE].astype(BF16)
        cb = lax.dot_general(cg, bg, (((1,), (1,)), ((), ())), preferred_element_type=F32)
        st = st_ref[g]
        y_off = jnp.dot(cg, st.astype(BF16), preferred_element_type=F32)
        ys, xscs, cds = [], [], []
        for pr in range(HEADS_PER_GROUP // 2):
            h0 = g * HEADS_PER_GROUP + 2 * pr
            x0 = h0 * SSD_HEAD_DIM
            xp = u[:, x0:x0 + LANES]
            xdt = xp * pair_cols(dt, h0)
            ms = []
            for h in (h0, h0 + 1):
                diff = a_cs[:, h:h + 1] - a_cs_t[h:h + 1, :]
                ms.append((cb * jnp.exp(jnp.where(tril, diff, -jnp.inf))).astype(BF16))
            m2 = jnp.concatenate(ms, axis=1)
            xd = jnp.concatenate([jnp.where(lo_half, xdt, 0.0), jnp.where(lo_half, 0.0, xdt)],
                                 axis=0).astype(BF16)
            y_diag = jnp.dot(m2, xd, preferred_element_type=F32)
            yo = y_off[:, 2 * pr * SSD_HEAD_DIM:2 * pr * SSD_HEAD_DIM + LANES]
            ys.append(y_diag + pair_cols(e_cs, h0) * yo + dskip_ref[:, x0:x0 + LANES] * xp)
            xscs.append((xdt * pair_cols(e_end, h0)).astype(BF16))
            cds.append(jnp.where(lane1 < SSD_HEAD_DIM, a_end[:, h0:h0 + 1], a_end[:, h0 + 1:h0 + 2]))
        xsc = jnp.concatenate(xscs, axis=1)
        new = lax.dot_general(bg, xsc, (((0,), (0,)), ((), ())), preferred_element_type=F32)
        st_ref[g] = st * jnp.exp(jnp.concatenate(cds, axis=1)) + new

        gw = slice(g * GROUP_WIDTH, (g + 1) * GROUP_WIDTH)
        yg = jnp.concatenate(ys, axis=1) * _silu(z_ref[:, gw].astype(F32))
        yg = yg * lax.rsqrt(jnp.mean(yg * yg, axis=-1, keepdims=True) + EPS)
        o_ref[:, gw] = (yg * nw_ref[:, gw]).astype(o_ref.dtype)


def _ssd(proj, small, cw, cb, dtb, alog, dskip, nw, *, batch, seq):
    L = SSD_CHUNK
    nc = seq // L
    conv_dim = cw.shape[1]
    row = lambda b, c: b * nc + c
    xbc_blk = (PROJ_COLS - conv_dim) // conv_dim
    z_blk = (PROJ_COLS - conv_dim - SSD_WIDTH) // SSD_WIDTH
    dt_blk = 1
    const = lambda b, c: (0, 0)
    return pl.pallas_call(
        _ssd_kernel,
        grid=(batch, nc),
        in_specs=[
            pl.BlockSpec((L, conv_dim), lambda b, c: (row(b, c), xbc_blk)),
            pl.BlockSpec((L, SSD_WIDTH), lambda b, c: (row(b, c), z_blk)),
            pl.BlockSpec((L, LANES), lambda b, c: (row(b, c), dt_blk)),
            pl.BlockSpec((SSD_CONV, conv_dim), const),
            pl.BlockSpec((1, conv_dim), const),
            pl.BlockSpec((1, LANES), const),
            pl.BlockSpec((1, LANES), const),
            pl.BlockSpec((1, SSD_WIDTH), const),
            pl.BlockSpec((1, SSD_WIDTH), const),
        ],
        out_specs=pl.BlockSpec((L, SSD_WIDTH), lambda b, c: (row(b, c), 0)),
        out_shape=jax.ShapeDtypeStruct((batch * seq, SSD_WIDTH), BF16),
        scratch_shapes=[
            pltpu.VMEM((CONV_HALO + L, conv_dim), F32),
            pltpu.VMEM((SSD_GROUPS, SSD_STATE, GROUP_WIDTH), F32),
        ],
        compiler_params=_params(("parallel", "arbitrary")),
        name="ssd",
    )(proj, proj, small, cw, cb, dtb, alog, dskip, nw)


def _outproj_kernel(ya_ref, ys_ref, wa_ref, ws_ref, x_ref, o_ref):
    o_ref[...] = (x_ref[...]
                  + jnp.dot(ya_ref[...], wa_ref[...], preferred_element_type=F32)
                  + jnp.dot(ys_ref[...], ws_ref[...], preferred_element_type=F32))


def _outproj(ya, ys, w, x, layer, *, tm=1024, tn=512):
    m, d = x.shape
    ka, ks = ya.shape[1], ys.shape[1]
    assert ka == ks
    tm, tn = min(tm, m), min(tn, d)
    return pl.pallas_call(
        _outproj_kernel,
        grid=(m // tm, d // tn),
        in_specs=[
            pl.BlockSpec((tm, ka), lambda i, j: (i, 0)),
            pl.BlockSpec((tm, ks), lambda i, j: (i, 0)),
            pl.BlockSpec((None, ka, tn), lambda i, j: (layer, 0, j)),
            pl.BlockSpec((None, ks, tn), lambda i, j: (layer, 1, j)),
            pl.BlockSpec((tm, tn), lambda i, j: (i, j)),
        ],
        out_specs=pl.BlockSpec((tm, tn), lambda i, j: (i, j)),
        out_shape=jax.ShapeDtypeStruct((m, d), F32),
        compiler_params=_params(("parallel", "arbitrary")),
        name="outproj",
    )(ya, ys, w, w, x)


def _rotate_half_cols(w):
    half = w.shape[-1] // 2
    return jnp.concatenate([-w[..., half:], w[..., :half]], axis=-1)


def _prep_w_in(w_in):
    lead = w_in.shape[:-1]
    o1 = Q_LORA
    o2 = o1 + KV_LORA
    o3 = o2 + QK_ROPE
    o4 = o3 + SSD_WIDTH
    o5 = w_in.shape[-1] - SSD_HEADS
    kpe = w_in[..., o2:o3]
    zeros = lambda n: jnp.zeros(lead + (n,), w_in.dtype)
    return jnp.concatenate(
        [w_in[..., :o1], zeros(Q_LORA_PAD - Q_LORA), w_in[..., o1:o2],
         kpe, _rotate_half_cols(kpe), w_in[..., o5:], zeros(SMALL_COLS - 2 * QK_ROPE - SSD_HEADS),
         w_in[..., o3:o4], w_in[..., o4:o5]], axis=-1).astype(BF16)


def _prep_w_q(w):
    lead = w.shape[:-1]
    w = w.reshape(lead + (MLA_HEADS, QK_NOPE + QK_ROPE))
    rope = w[..., QK_NOPE:]
    rope = jnp.concatenate([rope, _rotate_half_cols(rope)], axis=-1)
    return jnp.concatenate([w[..., :QK_NOPE].reshape(lead + (-1,)), rope.reshape(lead + (-1,))],
                           axis=-1).astype(BF16)


def _prep_w_kv(w):
    lead = w.shape[:-1]
    w = w.reshape(lead + (MLA_HEADS, QK_NOPE + V_HEAD))
    return jnp.concatenate([w[..., :QK_NOPE].reshape(lead + (-1,)), w[..., QK_NOPE:].reshape(lead + (-1,))],
                           axis=-1).astype(BF16)


def _pad_lanes(v):
    return jnp.pad(v, (0, LANES - v.shape[0])).reshape(1, LANES)


def kernel(x, positions, ffn1_norm, ffn1_w_gate, ffn1_w_up, ffn1_w_down, mix_norm, w_in, q_a_norm, w_q_up, kv_a_norm, w_kv_up, conv_w, conv_b, dt_bias, a_log, d_skip, ssd_norm, w_out, ffn2_norm, ffn2_w_gate, ffn2_w_up, ffn2_w_down, final_norm):
    batch, seq, d = x.shape
    depth = w_in.shape[0]
    m = batch * seq
    xs = x.reshape(m, d)
    pos = positions.reshape(m, 1)
    inv_freq = ROPE_THETA ** (-jnp.arange(0, QK_ROPE, 2, dtype=F32) / QK_ROPE)
    freq = jnp.tile(inv_freq, LANES // inv_freq.shape[0]).reshape(1, LANES)

    w1g, w1u, w1d = ffn1_w_gate.astype(BF16), ffn1_w_up.astype(BF16), ffn1_w_down.astype(BF16)
    w2g, w2u, w2d = ffn2_w_gate.astype(BF16), ffn2_w_up.astype(BF16), ffn2_w_down.astype(BF16)
    w_in_b, w_q_b, w_kv_b = _prep_w_in(w_in), _prep_w_q(w_q_up), _prep_w_kv(w_kv_up)
    w_out_b = w_out.astype(BF16)

    for l in range(depth):
        xs = _ffn(xs, ffn1_norm[l], w1g, w1u, w1d, l)
        proj, small = _inproj(xs, mix_norm[l], w_in_b, l)
        q, k, vt = _qkv(proj, small, pos, freq, q_a_norm[l], kv_a_norm[l], w_q_b, w_kv_b, l,
                        batch=batch, seq=seq)
        y_attn = _attention(q, k, vt).reshape(m, MLA_WIDTH)
        y_ssd = _ssd(proj, small, conv_w[l], conv_b[l].reshape(1, -1), _pad_lanes(dt_bias[l]),
                     _pad_lanes(a_log[l]), jnp.repeat(d_skip[l], SSD_HEAD_DIM).reshape(1, -1),
                     ssd_norm[l].reshape(1, -1), batch=batch, seq=seq)
        xs = _outproj(y_attn, y_ssd, w_out_b, xs, l)
        xs = _ffn(xs, ffn2_norm[l], w2g, w2u, w2d, l, final_norm if l == depth - 1 else None)
    return xs.reshape(batch, seq, d)
```

```python
import functools
import math

import jax
import jax.numpy as jnp
from jax import lax
from jax.experimental import pallas as pl
from jax.experimental.pallas import tpu as pltpu

F32 = jnp.float32
BF16 = jnp.bfloat16

MLA_HEADS = 16
Q_LORA = 896
KV_LORA = 512
QK_NOPE = 128
QK_ROPE = 64
V_HEAD = 128
ROPE_THETA = 10000.0
SSD_HEAD_DIM = 64
SSD_HEADS = 32
SSD_GROUPS = 8
SSD_STATE = 128
SSD_CONV = 4
SSD_CHUNK = 128
SSD_WIDTH = SSD_HEADS * SSD_HEAD_DIM
MLA_WIDTH = MLA_HEADS * V_HEAD
EPS = 1e-6

LANES = 128
Q_LORA_PAD = 1024
SMALL_COLS = 512
QK_DIM = 256
PROJ_COLS = Q_LORA_PAD + KV_LORA + SMALL_COLS + SSD_WIDTH + (SSD_WIDTH + 2 * SSD_GROUPS * SSD_STATE)

ATTN_TILE = 512
Q_SCALE = (QK_NOPE + QK_ROPE) ** -0.5 * math.log2(math.e)

DOWN_CHUNK = 1024

NORM_ROWS = 128

VMEM_LIMIT = 60 * 1024 * 1024


def _params(sem):
    return pltpu.CompilerParams(dimension_semantics=sem, vmem_limit_bytes=VMEM_LIMIT)


def _rms(x, g):
    return x * lax.rsqrt(jnp.mean(x * x, axis=-1, keepdims=True) + EPS) * g


def _row_chunks(rows):
    step = min(rows, NORM_ROWS)
    return [slice(r, r + step) for r in range(0, rows, step)]


def _silu(x):
    return x * jax.nn.sigmoid(x)


def _ffn_kernel(x_ref, g_ref, wg_ref, wu_ref, wd_ref, *rest, final):
    if final:
        fn_ref, o_ref, h_ref = rest
    else:
        o_ref, h_ref = rest
    f = pl.program_id(1)

    @pl.when(f == 0)
    def _():
        for rows in _row_chunks(x_ref.shape[0]):
            x = x_ref[rows, :]
            h_ref[rows, :] = _rms(x, g_ref[...]).astype(BF16)
            o_ref[rows, :] = x

    h = h_ref[...]
    g = jnp.dot(h, wg_ref[...], preferred_element_type=F32)
    u = jnp.dot(h, wu_ref[...], preferred_element_type=F32)
    a = (0.5 * _silu(g) * u).astype(BF16)
    for c0 in range(0, o_ref.shape[1], DOWN_CHUNK):
        cols = slice(c0, c0 + DOWN_CHUNK)
        o_ref[:, cols] += jnp.dot(a, wd_ref[:, cols], preferred_element_type=F32)

    if final:
        @pl.when(f == pl.num_programs(1) - 1)
        def _():
            for rows in _row_chunks(o_ref.shape[0]):
                o_ref[rows, :] = _rms(o_ref[rows, :], fn_ref[...])


def _ffn(x, g, wg, wu, wd, layer, final_norm=None, *, tm=512, tf=512):
    m, d = x.shape
    ff = wg.shape[2]
    tm, tf = min(tm, m), min(tf, ff)
    final = final_norm is not None
    in_specs = [
        pl.BlockSpec((tm, d), lambda i, f: (i, 0), pipeline_mode=pl.Buffered(1)),
        pl.BlockSpec((1, d), lambda i, f: (0, 0)),
        pl.BlockSpec((None, d, tf), lambda i, f: (layer, 0, f)),
        pl.BlockSpec((None, d, tf), lambda i, f: (layer, 0, f)),
        pl.BlockSpec((None, tf, d), lambda i, f: (layer, f, 0)),
    ]
    args = [x, g.reshape(1, d), wg, wu, wd]
    if final:
        in_specs.append(pl.BlockSpec((1, d), lambda i, f: (0, 0)))
        args.append(final_norm.reshape(1, d))
    return pl.pallas_call(
        functools.partial(_ffn_kernel, final=final),
        grid=(m // tm, ff // tf),
        in_specs=in_specs,
        out_specs=pl.BlockSpec((tm, d), lambda i, f: (i, 0)),
        out_shape=jax.ShapeDtypeStruct((m, d), F32),
        scratch_shapes=[pltpu.VMEM((tm, d), BF16)],
        compiler_params=_params(("parallel", "arbitrary")),
        name="ffn",
    )(*args)


SMALL_BLOCK = (Q_LORA_PAD + KV_LORA) // SMALL_COLS


def _inproj_kernel(x_ref, g_ref, w_ref, o_ref, small_ref, h_ref):
    j = pl.program_id(1)

    @pl.when(j == 0)
    def _():
        for rows in _row_chunks(x_ref.shape[0]):
            h_ref[rows, :] = _rms(x_ref[rows, :], g_ref[...]).astype(BF16)

    y = jnp.dot(h_ref[...], w_ref[...], preferred_element_type=F32)
    o_ref[...] = y.astype(BF16)

    @pl.when(j == SMALL_BLOCK)
    def _():
        small_ref[...] = y


def _inproj(x, g, w, layer, *, tm=1024):
    m, d = x.shape
    n = w.shape[2]
    tm, tn = min(tm, m), SMALL_COLS
    return pl.pallas_call(
        _inproj_kernel,
        grid=(m // tm, n // tn),
        in_specs=[
            pl.BlockSpec((tm, d), lambda i, j: (i, 0), pipeline_mode=pl.Buffered(1)),
            pl.BlockSpec((1, d), lambda i, j: (0, 0)),
            pl.BlockSpec((None, d, tn), lambda i, j: (layer, 0, j)),
        ],
        out_specs=[
            pl.BlockSpec((tm, tn), lambda i, j: (i, j)),
            pl.BlockSpec((tm, tn), lambda i, j: (i, 0)),
        ],
        out_shape=[
            jax.ShapeDtypeStruct((m, n), BF16),
            jax.ShapeDtypeStruct((m, tn), F32),
        ],
        scratch_shapes=[pltpu.VMEM((tm, d), BF16)],
        compiler_params=_params(("parallel", "arbitrary")),
        name="inproj",
    )(x, g.reshape(1, d), w)


def _rope_fold(t):
    return t + pltpu.roll(t, QK_ROPE, axis=1)


def _qkv_kernel(cq_ref, ckv_ref, kpe_ref, pos_ref, freq_ref, qn_ref, kvn_ref, wq_ref, wkv_ref,
                q_ref, k_ref, vt_ref):
    hq = _rms(cq_ref[:, :Q_LORA].astype(F32), qn_ref[...]).astype(BF16)
    hkv = _rms(ckv_ref[...].astype(F32), kvn_ref[...]).astype(BF16)
    yq = jnp.dot(hq, wq_ref[...], preferred_element_type=F32)
    ykv = jnp.dot(hkv, wkv_ref[...], preferred_element_type=F32)

    ang = pos_ref[...].astype(F32) * freq_ref[...]
    lane = lax.broadcasted_iota(jnp.int32, ang.shape, 1)
    first_half = lane < QK_ROPE
    table = jnp.where(first_half, jnp.cos(ang), jnp.sin(ang))

    k_rope = jnp.where(first_half, _rope_fold(kpe_ref[...] * table), 0.0).astype(BF16)
    yq = yq * Q_SCALE
    nope_w = MLA_HEADS * QK_NOPE
    for h in range(MLA_HEADS):
        lo = h * LANES
        q_ref[0, h, :, :QK_NOPE] = yq[:, lo:lo + QK_NOPE].astype(BF16)
        q_ref[0, h, :, QK_NOPE:] = _rope_fold(yq[:, nope_w + lo:nope_w + lo + LANES] * table).astype(BF16)
        k_ref[0, h, :, :QK_NOPE] = ykv[:, lo:lo + QK_NOPE].astype(BF16)
        k_ref[0, h, :, QK_NOPE:] = k_rope
        vt_ref[0, h, 0] = ykv[:, nope_w + lo:nope_w + lo + V_HEAD].T.astype(BF16)


def _qkv(proj, small, pos, freq, qn, kvn, wq, wkv, layer, *, batch, seq):
    tm = min(ATTN_TILE, seq)
    nt = seq // tm
    row = lambda b, s: b * nt + s
    kv_blk = Q_LORA_PAD // KV_LORA
    hd = MLA_HEADS
    return pl.pallas_call(
        _qkv_kernel,
        grid=(batch, nt),
        in_specs=[
            pl.BlockSpec((tm, Q_LORA_PAD), lambda b, s: (row(b, s), 0)),
            pl.BlockSpec((tm, KV_LORA), lambda b, s: (row(b, s), kv_blk)),
            pl.BlockSpec((tm, LANES), lambda b, s: (row(b, s), 0)),
            pl.BlockSpec((tm, 1), lambda b, s: (row(b, s), 0)),
            pl.BlockSpec((1, LANES), lambda b, s: (0, 0)),
            pl.BlockSpec((1, Q_LORA), lambda b, s: (0, 0)),
            pl.BlockSpec((1, KV_LORA), lambda b, s: (0, 0)),
            pl.BlockSpec((None,) + wq.shape[1:], lambda b, s: (layer, 0, 0)),
            pl.BlockSpec((None,) + wkv.shape[1:], lambda b, s: (layer, 0, 0)),
        ],
        out_specs=[
            pl.BlockSpec((1, hd, tm, QK_DIM), lambda b, s: (b, 0, s, 0)),
            pl.BlockSpec((1, hd, tm, QK_DIM), lambda b, s: (b, 0, s, 0)),
            pl.BlockSpec((1, hd, 1, V_HEAD, tm), lambda b, s: (b, 0, s, 0, 0)),
        ],
        out_shape=[
            jax.ShapeDtypeStruct((batch, hd, seq, QK_DIM), BF16),
            jax.ShapeDtypeStruct((batch, hd, seq, QK_DIM), BF16),
            jax.ShapeDtypeStruct((batch, hd, nt, V_HEAD, tm), BF16),
        ],
        compiler_params=_params(("parallel", "parallel")),
        name="qkv_up",
    )(proj, proj, small, pos, freq, qn.reshape(1, -1), kvn.reshape(1, -1), wq, wkv)


def _attn_kernel(q_ref, k_ref, vt_ref, o_ref, s0_ref, s1_ref, m_ref, l_ref, acc_ref, *, nt, tq):
    s_refs = (s0_ref, s1_ref)
    pairs = [(qi, j) for qi in range(nt) for j in range(qi + 1)]

    def scores_into(s_ref, qi, j):
        q = q_ref[0, 0, qi * tq:(qi + 1) * tq, :]
        k = k_ref[0, 0, j * tq:(j + 1) * tq, :]
        s_ref[...] = lax.dot_general(k, q, (((1,), (1,)), ((), ())), preferred_element_type=F32)

    scores_into(s_refs[0], *pairs[0])
    for t, (qi, j) in enumerate(pairs):
        if t + 1 < len(pairs):
            scores_into(s_refs[(t + 1) % 2], *pairs[t + 1])
        s = s_refs[t % 2][...]
        if j == qi:
            key = lax.broadcasted_iota(jnp.int32, s.shape, 0)
            query = lax.broadcasted_iota(jnp.int32, s.shape, 1)
            s = jnp.where(query >= key, s, -jnp.inf)
        m_tile = jnp.max(s, axis=0, keepdims=True)
        if j == 0:
            m_new = m_tile
        else:
            m_old = m_ref[...]
            m_new = jnp.maximum(m_old, m_tile)
            alpha = jnp.exp2(m_old - m_new)
        p = jnp.exp2(s - m_new)
        l_tile = jnp.sum(p, axis=0, keepdims=True)
        pv = jnp.dot(vt_ref[0, 0, j], p.astype(BF16), preferred_element_type=F32)
        if j == 0:
            l_ref[...] = l_tile
            acc_ref[...] = pv
        else:
            l_ref[...] = alpha * l_ref[...] + l_tile
            acc_ref[...] = alpha * acc_ref[...] + pv
        m_ref[...] = m_new
        if j == qi:
            o_ref[0, qi * tq:(qi + 1) * tq, :] = (acc_ref[...] / l_ref[...]).T.astype(o_ref.dtype)


def _attention(q, k, vt):
    b, hd, s, _ = q.shape
    nt, tq = vt.shape[2], vt.shape[4]
    return pl.pallas_call(
        functools.partial(_attn_kernel, nt=nt, tq=tq),
        grid=(b, hd),
        in_specs=[
            pl.BlockSpec((1, 1, s, QK_DIM), lambda bi, h: (bi, h, 0, 0)),
            pl.BlockSpec((1, 1, s, QK_DIM), lambda bi, h: (bi, h, 0, 0)),
            pl.BlockSpec((1, 1, nt, V_HEAD, tq), lambda bi, h: (bi, h, 0, 0, 0)),
        ],
        out_specs=pl.BlockSpec((1, s, V_HEAD), lambda bi, h: (bi, 0, h)),
        out_shape=jax.ShapeDtypeStruct((b, s, hd * V_HEAD), BF16),
        scratch_shapes=[
            pltpu.VMEM((tq, tq), F32),
            pltpu.VMEM((tq, tq), F32),
            pltpu.VMEM((1, tq), F32),
            pltpu.VMEM((1, tq), F32),
            pltpu.VMEM((V_HEAD, tq), F32),
        ],
        compiler_params=_params(("parallel", "parallel")),
        name="mla_attention",
    )(q, k, vt)


CONV_HALO = 8
HEADS_PER_GROUP = SSD_HEADS // SSD_GROUPS
GROUP_WIDTH = HEADS_PER_GROUP * SSD_HEAD_DIM


def _ssd_kernel(xbc_ref, z_ref, dt_ref, cw_ref, cb_ref, dtb_ref, alog_ref, dskip_ref, nw_ref,
                o_ref, ext_ref, st_ref):
    L = SSD_CHUNK
    c = pl.program_id(1)

    @pl.when(c == 0)
    def _():
        ext_ref[0:CONV_HALO, :] = jnp.zeros((CONV_HALO, ext_ref.shape[1]), F32)
        st_ref[...] = jnp.zeros(st_ref.shape, F32)

    @pl.when(c > 0)
    def _():
        ext_ref[0:CONV_HALO, :] = ext_ref[L:L + CONV_HALO, :]

    ext_ref[CONV_HALO:CONV_HALO + L, :] = xbc_ref[...].astype(F32)
    conv = cb_ref[...]
    for t in range(SSD_CONV):
        off = CONV_HALO - (SSD_CONV - 1) + t
        conv = conv + cw_ref[t:t + 1, :] * ext_ref[off:off + L, :]
    u = _silu(conv)

    stats = _ssd_decays(dt_ref, dtb_ref, alog_ref)
    tril = (lax.broadcasted_iota(jnp.int32, (L, L), 0) >= lax.broadcasted_iota(jnp.int32, (L, L), 1))
    lo_half = lax.broadcasted_iota(jnp.int32, (L, LANES), 1) < SSD_HEAD_DIM
    lo_half_row = lax.broadcasted_iota(jnp.int32, (1, LANES), 1) < SSD_HEAD_DIM
    for g in range(SSD_GROUPS):
        _ssd_group(g, stats, (tril, lo_half, lo_half_row), u, z_ref, dskip_ref, nw_ref, o_ref, st_ref)


def _ssd_decays(dt_ref, dtb_ref, alog_ref):
    L = SSD_CHUNK
    dt = jax.nn.softplus(dt_ref[...] + dtb_ref[...])
    a = dt * (-jnp.exp(alog_ref[...]))
    row = lax.broadcasted_iota(jnp.int32, (L, L), 0)
    col = lax.broadcasted_iota(jnp.int32, (L, L), 1)
    a_cs = jnp.dot((row >= col).astype(F32), a, preferred_element_type=F32,
                   precision=lax.Precision.HIGHEST)
    a_cs_t = jnp.dot(a.T, (row <= col).astype(F32), preferred_element_type=F32,
                     precision=lax.Precision.HIGHEST)
    a_end = a_cs[L - 1:L, :]
    e_cs = jnp.exp(a_cs)
    e_end = jnp.exp(a_end - a_cs)
    return dt, a_cs, a_cs_t, a_end, e_cs, e_end


def _ssd_group(g, stats, masks, u, z_ref, dskip_ref, nw_ref, o_ref, st_ref):
    dt, a_cs, a_cs_t, a_end, e_cs, e_end = stats
    tril, lo_half, lo_half_row = masks
    b_off = SSD_WIDTH
    c_off = SSD_WIDTH + SSD_GROUPS * SSD_STATE

    def pair_cols(v, h0):
        return jnp.where(lo_half, v[:, h0:h0 + 1], v[:, h0 + 1:h0 + 2])

    bg = u[:, b_off + g * SSD_STATE:b_off + (g + 1) * SSD_STATE].astype(BF16)
    cg = u[:, c_off + g * SSD_STATE:c_off + (g + 1) * SSD_STATE].astype(BF16)
    cb = lax.dot_general(cg, bg, (((1,), (1,)), ((), ())), preferred_element_type=F32)
    st = st_ref[g]
    y_off = jnp.dot(cg, st.astype(BF16), preferred_element_type=F32)
    ys, xscs, cds = [], [], []
    for pr in range(HEADS_PER_GROUP // 2):
        h0 = g * HEADS_PER_GROUP + 2 * pr
        x0 = h0 * SSD_HEAD_DIM
        xp = u[:, x0:x0 + LANES]
        xdt = xp * pair_cols(dt, h0)
        ms = []
        for h in (h0, h0 + 1):
            diff = a_cs[:, h:h + 1] - a_cs_t[h:h + 1, :]
            ms.append((cb * jnp.exp(jnp.where(tril, diff, -jnp.inf))).astype(BF16))
        m2 = jnp.concatenate(ms, axis=1)
        xd = jnp.concatenate([jnp.where(lo_half, xdt, 0.0), jnp.where(lo_half, 0.0, xdt)],
                             axis=0).astype(BF16)
        y_diag = jnp.dot(m2, xd, preferred_element_type=F32)
        yo = y_off[:, 2 * pr * SSD_HEAD_DIM:2 * pr * SSD_HEAD_DIM + LANES]
        ys.append(y_diag + pair_cols(e_cs, h0) * yo + dskip_ref[:, x0:x0 + LANES] * xp)
        xscs.append((xdt * pair_cols(e_end, h0)).astype(BF16))
        cds.append(jnp.where(lo_half_row, a_end[:, h0:h0 + 1], a_end[:, h0 + 1:h0 + 2]))
    xsc = jnp.concatenate(xscs, axis=1)
    new = lax.dot_general(bg, xsc, (((0,), (0,)), ((), ())), preferred_element_type=F32)
    st_ref[g] = st * jnp.exp(jnp.concatenate(cds, axis=1)) + new

    gw = slice(g * GROUP_WIDTH, (g + 1) * GROUP_WIDTH)
    yg = jnp.concatenate(ys, axis=1) * _silu(z_ref[:, gw].astype(F32))
    yg = yg * lax.rsqrt(jnp.mean(yg * yg, axis=-1, keepdims=True) + EPS)
    o_ref[:, gw] = (yg * nw_ref[:, gw]).astype(o_ref.dtype)


def _ssd(proj, small, cw, cb, dtb, alog, dskip, nw, *, batch, seq):
    L = SSD_CHUNK
    nc = seq // L
    conv_dim = cw.shape[1]
    row = lambda b, c: b * nc + c
    xbc_blk = (PROJ_COLS - conv_dim) // conv_dim
    z_blk = (PROJ_COLS - conv_dim - SSD_WIDTH) // SSD_WIDTH
    dt_blk = 1
    const = lambda b, c: (0, 0)
    return pl.pallas_call(
        _ssd_kernel,
        grid=(batch, nc),
        in_specs=[
            pl.BlockSpec((L, conv_dim), lambda b, c: (row(b, c), xbc_blk)),
            pl.BlockSpec((L, SSD_WIDTH), lambda b, c: (row(b, c), z_blk)),
            pl.BlockSpec((L, LANES), lambda b, c: (row(b, c), dt_blk)),
            pl.BlockSpec((SSD_CONV, conv_dim), const),
            pl.BlockSpec((1, conv_dim), const),
            pl.BlockSpec((1, LANES), const),
            pl.BlockSpec((1, LANES), const),
            pl.BlockSpec((1, SSD_WIDTH), const),
            pl.BlockSpec((1, SSD_WIDTH), const),
        ],
        out_specs=pl.BlockSpec((L, SSD_WIDTH), lambda b, c: (row(b, c), 0)),
        out_shape=jax.ShapeDtypeStruct((batch * seq, SSD_WIDTH), BF16),
        scratch_shapes=[
            pltpu.VMEM((CONV_HALO + L, conv_dim), F32),
            pltpu.VMEM((SSD_GROUPS, SSD_STATE, GROUP_WIDTH), F32),
        ],
        compiler_params=_params(("parallel", "arbitrary")),
        name="ssd",
    )(proj, proj, small, cw, cb, dtb, alog, dskip, nw)


def _outproj_kernel(ya_ref, ys_ref, wa_ref, ws_ref, x_ref, o_ref):
    o_ref[...] = (x_ref[...]
                  + jnp.dot(ya_ref[...], wa_ref[...], preferred_element_type=F32)
                  + jnp.dot(ys_ref[...], ws_ref[...], preferred_element_type=F32))


def _outproj(ya, ys, w, x, layer, *, tm=1024, tn=512):
    m, d = x.shape
    ka, ks = ya.shape[1], ys.shape[1]
    assert ka == ks
    tm, tn = min(tm, m), min(tn, d)
    return pl.pallas_call(
        _outproj_kernel,
        grid=(m // tm, d // tn),
        in_specs=[
            pl.BlockSpec((tm, ka), lambda i, j: (i, 0)),
            pl.BlockSpec((tm, ks), lambda i, j: (i, 0)),
            pl.BlockSpec((None, ka, tn), lambda i, j: (layer, 0, j)),
            pl.BlockSpec((None, ks, tn), lambda i, j: (layer, 1, j)),
            pl.BlockSpec((tm, tn), lambda i, j: (i, j)),
        ],
        out_specs=pl.BlockSpec((tm, tn), lambda i, j: (i, j)),
        out_shape=jax.ShapeDtypeStruct((m, d), F32),
        compiler_params=_params(("parallel", "arbitrary")),
        name="outproj",
    )(ya, ys, w, w, x)


def _rotate_half_cols(w):
    half = w.shape[-1] // 2
    return jnp.concatenate([-w[..., half:], w[..., :half]], axis=-1)


def _prep_w_in(w_in):
    lead = w_in.shape[:-1]
    o1 = Q_LORA
    o2 = o1 + KV_LORA
    o3 = o2 + QK_ROPE
    o4 = o3 + SSD_WIDTH
    o5 = w_in.shape[-1] - SSD_HEADS
    kpe = w_in[..., o2:o3]
    zeros = lambda n: jnp.zeros(lead + (n,), w_in.dtype)
    return jnp.concatenate(
        [w_in[..., :o1], zeros(Q_LORA_PAD - Q_LORA), w_in[..., o1:o2],
         kpe, _rotate_half_cols(kpe), w_in[..., o5:], zeros(SMALL_COLS - 2 * QK_ROPE - SSD_HEADS),
         w_in[..., o3:o4], w_in[..., o4:o5]], axis=-1).astype(BF16)


def _prep_w_q(w):
    lead = w.shape[:-1]
    w = w.reshape(lead + (MLA_HEADS, QK_NOPE + QK_ROPE))
    rope = w[..., QK_NOPE:]
    rope = jnp.concatenate([rope, _rotate_half_cols(rope)], axis=-1)
    return jnp.concatenate([w[..., :QK_NOPE].reshape(lead + (-1,)), rope.reshape(lead + (-1,))],
                           axis=-1).astype(BF16)


def _prep_w_kv(w):
    lead = w.shape[:-1]
    w = w.reshape(lead + (MLA_HEADS, QK_NOPE + V_HEAD))
    return jnp.concatenate([w[..., :QK_NOPE].reshape(lead + (-1,)), w[..., QK_NOPE:].reshape(lead + (-1,))],
                           axis=-1).astype(BF16)


def _pad_lanes(v):
    return jnp.pad(v, (0, LANES - v.shape[0])).reshape(1, LANES)


def kernel(x, positions, ffn1_norm, ffn1_w_gate, ffn1_w_up, ffn1_w_down, mix_norm, w_in, q_a_norm, w_q_up, kv_a_norm, w_kv_up, conv_w, conv_b, dt_bias, a_log, d_skip, ssd_norm, w_out, ffn2_norm, ffn2_w_gate, ffn2_w_up, ffn2_w_down, final_norm):
    batch, seq, d = x.shape
    depth = w_in.shape[0]
    m = batch * seq
    xs = x.reshape(m, d)
    pos = positions.reshape(m, 1)
    inv_freq = ROPE_THETA ** (-jnp.arange(0, QK_ROPE, 2, dtype=F32) / QK_ROPE)
    freq = jnp.tile(inv_freq, LANES // inv_freq.shape[0]).reshape(1, LANES)

    w1g, w1u, w1d = ffn1_w_gate.astype(BF16), ffn1_w_up.astype(BF16), ffn1_w_down.astype(BF16)
    w2g, w2u, w2d = ffn2_w_gate.astype(BF16), ffn2_w_up.astype(BF16), ffn2_w_down.astype(BF16)
    w_in_b, w_q_b, w_kv_b = _prep_w_in(w_in), _prep_w_q(w_q_up), _prep_w_kv(w_kv_up)
    w_out_b = w_out.astype(BF16)

    for l in range(depth):
        xs = _ffn(xs, ffn1_norm[l], w1g, w1u, w1d, l)
        proj, small = _inproj(xs, mix_norm[l], w_in_b, l)
        q, k, vt = _qkv(proj, small, pos, freq, q_a_norm[l], kv_a_norm[l], w_q_b, w_kv_b, l,
                        batch=batch, seq=seq)
        y_attn = _attention(q, k, vt).reshape(m, MLA_WIDTH)
        y_ssd = _ssd(proj, small, conv_w[l], conv_b[l].reshape(1, -1), _pad_lanes(dt_bias[l]),
                     _pad_lanes(a_log[l]), jnp.repeat(d_skip[l], SSD_HEAD_DIM).reshape(1, -1),
                     ssd_norm[l].reshape(1, -1), batch=batch, seq=seq)
        xs = _outproj(y_attn, y_ssd, w_out_b, xs, l)
        xs = _ffn(xs, ffn2_norm[l], w2g, w2u, w2d, l, final_norm if l == depth - 1 else None)
    return xs.reshape(batch, seq, d)
```

```python
import functools
import math

import jax
import jax.numpy as jnp
from jax import lax
from jax.experimental import pallas as pl
from jax.experimental.pallas import tpu as pltpu

F32 = jnp.float32
BF16 = jnp.bfloat16

MLA_HEADS = 16
Q_LORA = 896
KV_LORA = 512
QK_NOPE = 128
QK_ROPE = 64
V_HEAD = 128
ROPE_THETA = 10000.0
SSD_HEAD_DIM = 64
SSD_HEADS = 32
SSD_GROUPS = 8
SSD_STATE = 128
SSD_CONV = 4
SSD_CHUNK = 128
SSD_WIDTH = SSD_HEADS * SSD_HEAD_DIM
MLA_WIDTH = MLA_HEADS * V_HEAD
EPS = 1e-6

LANES = 128
Q_LORA_PAD = 1024
SMALL_COLS = 512
QK_DIM = 256
PROJ_COLS = Q_LORA_PAD + KV_LORA + SMALL_COLS + SSD_WIDTH + (SSD_WIDTH + 2 * SSD_GROUPS * SSD_STATE)

ATTN_TILE = 512
Q_SCALE = (QK_NOPE + QK_ROPE) ** -0.5 * math.log2(math.e)

DOWN_CHUNK = 1024

NORM_ROWS = 128

VMEM_LIMIT = 60 * 1024 * 1024


def _params(sem):
    return pltpu.CompilerParams(dimension_semantics=sem, vmem_limit_bytes=VMEM_LIMIT)


def _rms(x, g):
    return x * lax.rsqrt(jnp.mean(x * x, axis=-1, keepdims=True) + EPS) * g


def _row_chunks(rows):
    step = min(rows, NORM_ROWS)
    return [slice(r, r + step) for r in range(0, rows, step)]


def _silu(x):
    return x * jax.nn.sigmoid(x)


def _ffn_kernel(x_ref, g_ref, wg_ref, wu_ref, wd_ref, *rest, final):
    if final:
        fn_ref, o_ref, h_ref = rest
    else:
        o_ref, h_ref = rest
    f = pl.program_id(1)

    @pl.when(f == 0)
    def _():
        for rows in _row_chunks(x_ref.shape[0]):
            x = x_ref[rows, :]
            h_ref[rows, :] = _rms(x, g_ref[...]).astype(BF16)
            o_ref[rows, :] = x

    h = h_ref[...]
    g = jnp.dot(h, wg_ref[...], preferred_element_type=F32)
    u = jnp.dot(h, wu_ref[...], preferred_element_type=F32)
    a = (0.5 * _silu(g) * u).astype(BF16)
    for c0 in range(0, o_ref.shape[1], DOWN_CHUNK):
        cols = slice(c0, c0 + DOWN_CHUNK)
        o_ref[:, cols] += jnp.dot(a, wd_ref[:, cols], preferred_element_type=F32)

    if final:
        @pl.when(f == pl.num_programs(1) - 1)
        def _():
            for rows in _row_chunks(o_ref.shape[0]):
                o_ref[rows, :] = _rms(o_ref[rows, :], fn_ref[...])


def _ffn(x, g, wg, wu, wd, layer, final_norm=None, *, tm=512, tf=512):
    m, d = x.shape
    ff = wg.shape[2]
    tm, tf = min(tm, m), min(tf, ff)
    final = final_norm is not None
    in_specs = [
        pl.BlockSpec((tm, d), lambda i, f: (i, 0), pipeline_mode=pl.Buffered(1)),
        pl.BlockSpec((1, d), lambda i, f: (0, 0)),
        pl.BlockSpec((None, d, tf), lambda i, f: (layer, 0, f)),
        pl.BlockSpec((None, d, tf), lambda i, f: (layer, 0, f)),
        pl.BlockSpec((None, tf, d), lambda i, f: (layer, f, 0)),
    ]
    args = [x, g.reshape(1, d), wg, wu, wd]
    if final:
        in_specs.append(pl.BlockSpec((1, d), lambda i, f: (0, 0)))
        args.append(final_norm.reshape(1, d))
    return pl.pallas_call(
        functools.partial(_ffn_kernel, final=final),
        grid=(m // tm, ff // tf),
        in_specs=in_specs,
        out_specs=pl.BlockSpec((tm, d), lambda i, f: (i, 0)),
        out_shape=jax.ShapeDtypeStruct((m, d), F32),
        scratch_shapes=[pltpu.VMEM((tm, d), BF16)],
        compiler_params=_params(("parallel", "arbitrary")),
        name="ffn",
    )(*args)


PROJ_TILE = 1024
SMALL_START = Q_LORA_PAD + KV_LORA


def _inproj_kernel(x_ref, g_ref, w_ref, o_ref, small_ref, h_ref):
    j = pl.program_id(1)

    @pl.when(j == 0)
    def _():
        for rows in _row_chunks(x_ref.shape[0]):
            h_ref[rows, :] = _rms(x_ref[rows, :], g_ref[...]).astype(BF16)

    y = jnp.dot(h_ref[...], w_ref[...], preferred_element_type=F32)
    o_ref[...] = y.astype(BF16)

    @pl.when(j == SMALL_START // PROJ_TILE)
    def _():
        lo = SMALL_START % PROJ_TILE
        small_ref[...] = y[:, lo:lo + SMALL_COLS]


def _inproj(x, g, w, layer, *, tm=512):
    m, d = x.shape
    n = w.shape[2]
    tm, tn = min(tm, m), PROJ_TILE
    return pl.pallas_call(
        _inproj_kernel,
        grid=(m // tm, n // tn),
        in_specs=[
            pl.BlockSpec((tm, d), lambda i, j: (i, 0)),
            pl.BlockSpec((1, d), lambda i, j: (0, 0)),
            pl.BlockSpec((None, d, tn), lambda i, j: (layer, 0, j)),
        ],
        out_specs=[
            pl.BlockSpec((tm, tn), lambda i, j: (i, j)),
            pl.BlockSpec((tm, SMALL_COLS), lambda i, j: (i, 0)),
        ],
        out_shape=[
            jax.ShapeDtypeStruct((m, n), BF16),
            jax.ShapeDtypeStruct((m, SMALL_COLS), F32),
        ],
        scratch_shapes=[pltpu.VMEM((tm, d), BF16)],
        compiler_params=_params(("parallel", "arbitrary")),
        name="inproj",
    )(x, g.reshape(1, d), w)


def _rope_fold(t):
    return t + pltpu.roll(t, QK_ROPE, axis=1)


def _qkv_kernel(cq_ref, ckv_ref, kpe_ref, pos_ref, freq_ref, qn_ref, kvn_ref, wq_ref, wkv_ref,
                q_ref, k_ref, vt_ref):
    hq = _rms(cq_ref[:, :Q_LORA].astype(F32), qn_ref[...]).astype(BF16)
    hkv = _rms(ckv_ref[...].astype(F32), kvn_ref[...]).astype(BF16)
    yq = jnp.dot(hq, wq_ref[...], preferred_element_type=F32)
    ykv = jnp.dot(hkv, wkv_ref[...], preferred_element_type=F32)

    ang = pos_ref[...].astype(F32) * freq_ref[...]
    lane = lax.broadcasted_iota(jnp.int32, ang.shape, 1)
    first_half = lane < QK_ROPE
    table = jnp.where(first_half, jnp.cos(ang), jnp.sin(ang))

    k_rope = jnp.where(first_half, _rope_fold(kpe_ref[...] * table), 0.0).astype(BF16)
    yq = yq * Q_SCALE
    nope_w = MLA_HEADS * QK_NOPE
    for h in range(MLA_HEADS):
        lo = h * LANES
        q_ref[0, h, :, :QK_NOPE] = yq[:, lo:lo + QK_NOPE].astype(BF16)
        q_ref[0, h, :, QK_NOPE:] = _rope_fold(yq[:, nope_w + lo:nope_w + lo + LANES] * table).astype(BF16)
        kv_lo = h * (QK_NOPE + V_HEAD)
        k_ref[0, h, :, :QK_NOPE] = ykv[:, kv_lo:kv_lo + QK_NOPE].astype(BF16)
        k_ref[0, h, :, QK_NOPE:] = k_rope
        vt_ref[0, h, 0] = ykv[:, kv_lo + QK_NOPE:kv_lo + QK_NOPE + V_HEAD].T.astype(BF16)


def _qkv(proj, small, pos, freq, qn, kvn, wq, wkv, layer, *, batch, seq):
    tm = min(ATTN_TILE, seq)
    nt = seq // tm
    row = lambda b, s: b * nt + s
    kv_blk = Q_LORA_PAD // KV_LORA
    hd = MLA_HEADS
    return pl.pallas_call(
        _qkv_kernel,
        grid=(batch, nt),
        in_specs=[
            pl.BlockSpec((tm, Q_LORA_PAD), lambda b, s: (row(b, s), 0)),
            pl.BlockSpec((tm, KV_LORA), lambda b, s: (row(b, s), kv_blk)),
            pl.BlockSpec((tm, LANES), lambda b, s: (row(b, s), 0)),
            pl.BlockSpec((tm, 1), lambda b, s: (row(b, s), 0)),
            pl.BlockSpec((1, LANES), lambda b, s: (0, 0)),
            pl.BlockSpec((1, Q_LORA), lambda b, s: (0, 0)),
            pl.BlockSpec((1, KV_LORA), lambda b, s: (0, 0)),
            pl.BlockSpec((None,) + wq.shape[1:], lambda b, s: (layer, 0, 0)),
            pl.BlockSpec((None,) + wkv.shape[1:], lambda b, s: (layer, 0, 0)),
        ],
        out_specs=[
            pl.BlockSpec((1, hd, tm, QK_DIM), lambda b, s: (b, 0, s, 0)),
            pl.BlockSpec((1, hd, tm, QK_DIM), lambda b, s: (b, 0, s, 0)),
            pl.BlockSpec((1, hd, 1, V_HEAD, tm), lambda b, s: (b, 0, s, 0, 0)),
        ],
        out_shape=[
            jax.ShapeDtypeStruct((batch, hd, seq, QK_DIM), BF16),
            jax.ShapeDtypeStruct((batch, hd, seq, QK_DIM), BF16),
            jax.ShapeDtypeStruct((batch, hd, nt, V_HEAD, tm), BF16),
        ],
        compiler_params=_params(("parallel", "parallel")),
        name="qkv_up",
    )(proj, proj, small, pos, freq, qn.reshape(1, -1), kvn.reshape(1, -1), wq, wkv)


def _attn_kernel(q_ref, k_ref, vt_ref, o_ref, s0_ref, s1_ref, m_ref, l_ref, acc_ref, *, nt, tq):
    s_refs = (s0_ref, s1_ref)
    pairs = [(qi, j) for qi in range(nt) for j in range(qi + 1)]

    def scores_into(s_ref, qi, j):
        q = q_ref[0, 0, qi * tq:(qi + 1) * tq, :]
        k = k_ref[0, 0, j * tq:(j + 1) * tq, :]
        s_ref[...] = lax.dot_general(k, q, (((1,), (1,)), ((), ())), preferred_element_type=F32)

    scores_into(s_refs[0], *pairs[0])
    for t, (qi, j) in enumerate(pairs):
        if t + 1 < len(pairs):
            scores_into(s_refs[(t + 1) % 2], *pairs[t + 1])
        s = s_refs[t % 2][...]
        if j == qi:
            key = lax.broadcasted_iota(jnp.int32, s.shape, 0)
            query = lax.broadcasted_iota(jnp.int32, s.shape, 1)
            s = jnp.where(query >= key, s, -jnp.inf)
        m_tile = jnp.max(s, axis=0, keepdims=True)
        if j == 0:
            m_new = m_tile
        else:
            m_old = m_ref[...]
            m_new = jnp.maximum(m_old, m_tile)
            alpha = jnp.exp2(m_old - m_new)
        p = jnp.exp2(s - m_new)
        l_tile = jnp.sum(p, axis=0, keepdims=True)
        pv = jnp.dot(vt_ref[0, 0, j], p.astype(BF16), preferred_element_type=F32)
        if j == 0:
            l_ref[...] = l_tile
            acc_ref[...] = pv
        else:
            l_ref[...] = alpha * l_ref[...] + l_tile
            acc_ref[...] = alpha * acc_ref[...] + pv
        m_ref[...] = m_new
        if j == qi:
            o_ref[0, qi * tq:(qi + 1) * tq, :] = (acc_ref[...] / l_ref[...]).T.astype(o_ref.dtype)


def _attention(q, k, vt):
    b, hd, s, _ = q.shape
    nt, tq = vt.shape[2], vt.shape[4]
    return pl.pallas_call(
        functools.partial(_attn_kernel, nt=nt, tq=tq),
        grid=(b, hd),
        in_specs=[
            pl.BlockSpec((1, 1, s, QK_DIM), lambda bi, h: (bi, h, 0, 0)),
            pl.BlockSpec((1, 1, s, QK_DIM), lambda bi, h: (bi, h, 0, 0)),
            pl.BlockSpec((1, 1, nt, V_HEAD, tq), lambda bi, h: (bi, h, 0, 0, 0)),
        ],
        out_specs=pl.BlockSpec((1, s, V_HEAD), lambda bi, h: (bi, 0, h)),
        out_shape=jax.ShapeDtypeStruct((b, s, hd * V_HEAD), BF16),
        scratch_shapes=[
            pltpu.VMEM((tq, tq), F32),
            pltpu.VMEM((tq, tq), F32),
            pltpu.VMEM((1, tq), F32),
            pltpu.VMEM((1, tq), F32),
            pltpu.VMEM((V_HEAD, tq), F32),
        ],
        compiler_params=_params(("parallel", "parallel")),
        name="mla_attention",
    )(q, k, vt)


CONV_HALO = 8
HEADS_PER_GROUP = SSD_HEADS // SSD_GROUPS
GROUP_WIDTH = HEADS_PER_GROUP * SSD_HEAD_DIM


def _ssd_kernel(xbc_ref, z_ref, dt_ref, cw_ref, cb_ref, dtb_ref, alog_ref, dskip_ref, nw_ref,
                o_ref, ext_ref, st_ref):
    L = SSD_CHUNK
    c = pl.program_id(1)

    @pl.when(c == 0)
    def _():
        ext_ref[0:CONV_HALO, :] = jnp.zeros((CONV_HALO, ext_ref.shape[1]), F32)
        st_ref[...] = jnp.zeros(st_ref.shape, F32)

    @pl.when(c > 0)
    def _():
        ext_ref[0:CONV_HALO, :] = ext_ref[L:L + CONV_HALO, :]

    ext_ref[CONV_HALO:CONV_HALO + L, :] = xbc_ref[...].astype(F32)
    conv = cb_ref[...]
    for t in range(SSD_CONV):
        off = CONV_HALO - (SSD_CONV - 1) + t
        conv = conv + cw_ref[t:t + 1, :] * ext_ref[off:off + L, :]
    u = _silu(conv)

    stats = _ssd_decays(dt_ref, dtb_ref, alog_ref)
    tril = (lax.broadcasted_iota(jnp.int32, (L, L), 0) >= lax.broadcasted_iota(jnp.int32, (L, L), 1))
    lo_half = lax.broadcasted_iota(jnp.int32, (L, LANES), 1) < SSD_HEAD_DIM
    lo_half_row = lax.broadcasted_iota(jnp.int32, (1, LANES), 1) < SSD_HEAD_DIM
    for g in range(SSD_GROUPS):
        _ssd_group(g, stats, (tril, lo_half, lo_half_row), u, z_ref, dskip_ref, nw_ref, o_ref, st_ref)


def _ssd_decays(dt_ref, dtb_ref, alog_ref):
    L = SSD_CHUNK
    dt = jax.nn.softplus(dt_ref[...] + dtb_ref[...])
    a = dt * (-jnp.exp(alog_ref[...]))
    row = lax.broadcasted_iota(jnp.int32, (L, L), 0)
    col = lax.broadcasted_iota(jnp.int32, (L, L), 1)
    a_cs = jnp.dot((row >= col).astype(F32), a, preferred_element_type=F32,
                   precision=lax.Precision.HIGHEST)
    a_cs_t = jnp.dot(a.T, (row <= col).astype(F32), preferred_element_type=F32,
                     precision=lax.Precision.HIGHEST)
    a_end = a_cs[L - 1:L, :]
    e_cs = jnp.exp(a_cs)
    e_end = jnp.exp(a_end - a_cs)
    return dt, a_cs, a_cs_t, a_end, e_cs, e_end


def _ssd_group(g, stats, masks, u, z_ref, dskip_ref, nw_ref, o_ref, st_ref):
    dt, a_cs, a_cs_t, a_end, e_cs, e_end = stats
    tril, lo_half, lo_half_row = masks
    b_off = SSD_WIDTH
    c_off = SSD_WIDTH + SSD_GROUPS * SSD_STATE

    def pair_cols(v, h0):
        return jnp.where(lo_half, v[:, h0:h0 + 1], v[:, h0 + 1:h0 + 2])

    bg = u[:, b_off + g * SSD_STATE:b_off + (g + 1) * SSD_STATE].astype(BF16)
    cg = u[:, c_off + g * SSD_STATE:c_off + (g + 1) * SSD_STATE].astype(BF16)
    cb = lax.dot_general(cg, bg, (((1,), (1,)), ((), ())), preferred_element_type=F32)
    st = st_ref[g]
    y_off = jnp.dot(cg, st.astype(BF16), preferred_element_type=F32)
    ys, xscs, cds = [], [], []
    for pr in range(HEADS_PER_GROUP // 2):
        h0 = g * HEADS_PER_GROUP + 2 * pr
        x0 = h0 * SSD_HEAD_DIM
        xp = u[:, x0:x0 + LANES]
        xdt = xp * pair_cols(dt, h0)
        ms = []
        for h in (h0, h0 + 1):
            diff = a_cs[:, h:h + 1] - a_cs_t[h:h + 1, :]
            ms.append((cb * jnp.exp(jnp.where(tril, diff, -jnp.inf))).astype(BF16))
        m2 = jnp.concatenate(ms, axis=1)
        xd = jnp.concatenate([jnp.where(lo_half, xdt, 0.0), jnp.where(lo_half, 0.0, xdt)],
                             axis=0).astype(BF16)
        y_diag = jnp.dot(m2, xd, preferred_element_type=F32)
        yo = y_off[:, 2 * pr * SSD_HEAD_DIM:2 * pr * SSD_HEAD_DIM + LANES]
        ys.append(y_diag + pair_cols(e_cs, h0) * yo + dskip_ref[:, x0:x0 + LANES] * xp)
        xscs.append((xdt * pair_cols(e_end, h0)).astype(BF16))
        cds.append(jnp.where(lo_half_row, a_end[:, h0:h0 + 1], a_end[:, h0 + 1:h0 + 2]))
    xsc = jnp.concatenate(xscs, axis=1)
    new = lax.dot_general(bg, xsc, (((0,), (0,)), ((), ())), preferred_element_type=F32)
    st_ref[g] = st * jnp.exp(jnp.concatenate(cds, axis=1)) + new

    gw = slice(g * GROUP_WIDTH, (g + 1) * GROUP_WIDTH)
    yg = jnp.concatenate(ys, axis=1) * _silu(z_ref[:, gw].astype(F32))
    yg = yg * lax.rsqrt(jnp.mean(yg * yg, axis=-1, keepdims=True) + EPS)
    o_ref[:, gw] = (yg * nw_ref[:, gw]).astype(o_ref.dtype)


def _ssd(proj, small, cw, cb, dtb, alog, dskip, nw, *, batch, seq):
    L = SSD_CHUNK
    nc = seq // L
    conv_dim = cw.shape[1]
    row = lambda b, c: b * nc + c
    xbc_blk = (PROJ_COLS - conv_dim) // conv_dim
    z_blk = (PROJ_COLS - conv_dim - SSD_WIDTH) // SSD_WIDTH
    dt_blk = 1
    const = lambda b, c: (0, 0)
    return pl.pallas_call(
        _ssd_kernel,
        grid=(batch, nc),
        in_specs=[
            pl.BlockSpec((L, conv_dim), lambda b, c: (row(b, c), xbc_blk)),
            pl.BlockSpec((L, SSD_WIDTH), lambda b, c: (row(b, c), z_blk)),
            pl.BlockSpec((L, LANES), lambda b, c: (row(b, c), dt_blk)),
            pl.BlockSpec((SSD_CONV, conv_dim), const),
            pl.BlockSpec((1, conv_dim), const),
            pl.BlockSpec((1, LANES), const),
            pl.BlockSpec((1, LANES), const),
            pl.BlockSpec((1, SSD_WIDTH), const),
            pl.BlockSpec((1, SSD_WIDTH), const),
        ],
        out_specs=pl.BlockSpec((L, SSD_WIDTH), lambda b, c: (row(b, c), 0)),
        out_shape=jax.ShapeDtypeStruct((batch * seq, SSD_WIDTH), BF16),
        scratch_shapes=[
            pltpu.VMEM((CONV_HALO + L, conv_dim), F32),
            pltpu.VMEM((SSD_GROUPS, SSD_STATE, GROUP_WIDTH), F32),
        ],
        compiler_params=_params(("parallel", "arbitrary")),
        name="ssd",
    )(proj, proj, small, cw, cb, dtb, alog, dskip, nw)


def _outproj_kernel(ya_ref, ys_ref, wa_ref, ws_ref, x_ref, o_ref):
    o_ref[...] = (x_ref[...]
                  + jnp.dot(ya_ref[...], wa_ref[...], preferred_element_type=F32)
                  + jnp.dot(ys_ref[...], ws_ref[...], preferred_element_type=F32))


def _outproj(ya, ys, w, x, layer, *, tm=1024, tn=512):
    m, d = x.shape
    ka, ks = ya.shape[1], ys.shape[1]
    assert ka == ks
    tm, tn = min(tm, m), min(tn, d)
    return pl.pallas_call(
        _outproj_kernel,
        grid=(m // tm, d // tn),
        in_specs=[
            pl.BlockSpec((tm, ka), lambda i, j: (i, 0)),
            pl.BlockSpec((tm, ks), lambda i, j: (i, 0)),
            pl.BlockSpec((None, ka, tn), lambda i, j: (layer, 0, j)),
            pl.BlockSpec((None, ks, tn), lambda i, j: (layer, 1, j)),
            pl.BlockSpec((tm, tn), lambda i, j: (i, j)),
        ],
        out_specs=pl.BlockSpec((tm, tn), lambda i, j: (i, j)),
        out_shape=jax.ShapeDtypeStruct((m, d), F32),
        compiler_params=_params(("parallel", "arbitrary")),
        name="outproj",
    )(ya, ys, w, w, x)


def _rotate_half_cols(w):
    half = w.shape[-1] // 2
    return jnp.concatenate([-w[..., half:], w[..., :half]], axis=-1)


W_IN_ROWS = 256


def _relayout_w_in_kernel(w_ref, small_ref, o_ref):
    o1 = Q_LORA
    o2 = o1 + KV_LORA
    o3 = o2 + QK_ROPE
    o4 = o3 + SSD_WIDTH
    o5 = w_ref.shape[1] - SSD_HEADS
    z_start = SMALL_START + SMALL_COLS
    o_ref[:, :o1] = w_ref[:, :o1].astype(BF16)
    o_ref[:, o1:Q_LORA_PAD] = jnp.zeros((o_ref.shape[0], Q_LORA_PAD - o1), BF16)
    o_ref[:, Q_LORA_PAD:SMALL_START] = w_ref[:, o1:o2].astype(BF16)
    o_ref[:, SMALL_START:z_start] = small_ref[...]
    o_ref[:, z_start:z_start + SSD_WIDTH] = w_ref[:, o3:o4].astype(BF16)
    o_ref[:, z_start + SSD_WIDTH:] = w_ref[:, o4:o5].astype(BF16)


def _prep_w_in(w_in):
    depth, d, cols = w_in.shape
    o2 = Q_LORA + KV_LORA
    o3 = o2 + QK_ROPE
    kpe = w_in[..., o2:o3]
    small = jnp.concatenate(
        [kpe, _rotate_half_cols(kpe), w_in[..., cols - SSD_HEADS:],
         jnp.zeros((depth, d, SMALL_COLS - 2 * QK_ROPE - SSD_HEADS), w_in.dtype)], axis=-1).astype(BF16)
    tr = min(W_IN_ROWS, d)
    return pl.pallas_call(
        _relayout_w_in_kernel,
        grid=(depth, d // tr),
        in_specs=[
            pl.BlockSpec((None, tr, cols), lambda l, r: (l, r, 0)),
            pl.BlockSpec((None, tr, SMALL_COLS), lambda l, r: (l, r, 0)),
        ],
        out_specs=pl.BlockSpec((None, tr, PROJ_COLS), lambda l, r: (l, r, 0)),
        out_shape=jax.ShapeDtypeStruct((depth, d, PROJ_COLS), BF16),
        compiler_params=_params(("parallel", "parallel")),
        name="relayout_w_in",
    )(w_in, small)


def _prep_w_q(w):
    lead = w.shape[:-1]
    w = w.reshape(lead + (MLA_HEADS, QK_NOPE + QK_ROPE))
    rope = w[..., QK_NOPE:]
    rope = jnp.concatenate([rope, _rotate_half_cols(rope)], axis=-1)
    return jnp.concatenate([w[..., :QK_NOPE].reshape(lead + (-1,)), rope.reshape(lead + (-1,))],
                           axis=-1).astype(BF16)


def _pad_lanes(v):
    return jnp.pad(v, (0, LANES - v.shape[0])).reshape(1, LANES)


def kernel(x, positions, ffn1_norm, ffn1_w_gate, ffn1_w_up, ffn1_w_down, mix_norm, w_in, q_a_norm, w_q_up, kv_a_norm, w_kv_up, conv_w, conv_b, dt_bias, a_log, d_skip, ssd_norm, w_out, ffn2_norm, ffn2_w_gate, ffn2_w_up, ffn2_w_down, final_norm):
    batch, seq, d = x.shape
    depth = w_in.shape[0]
    m = batch * seq
    xs = x.reshape(m, d)
    pos = positions.reshape(m, 1)
    inv_freq = ROPE_THETA ** (-jnp.arange(0, QK_ROPE, 2, dtype=F32) / QK_ROPE)
    freq = jnp.tile(inv_freq, LANES // inv_freq.shape[0]).reshape(1, LANES)

    w1g, w1u, w1d = ffn1_w_gate.astype(BF16), ffn1_w_up.astype(BF16), ffn1_w_down.astype(BF16)
    w2g, w2u, w2d = ffn2_w_gate.astype(BF16), ffn2_w_up.astype(BF16), ffn2_w_down.astype(BF16)
    w_in_b, w_q_b, w_kv_b = _prep_w_in(w_in), _prep_w_q(w_q_up), w_kv_up.astype(BF16)
    w_out_b = w_out.astype(BF16)

    for l in range(depth):
        xs = _ffn(xs, ffn1_norm[l], w1g, w1u, w1d, l)
        proj, small = _inproj(xs, mix_norm[l], w_in_b, l)
        q, k, vt = _qkv(proj, small, pos, freq, q_a_norm[l], kv_a_norm[l], w_q_b, w_kv_b, l,
                        batch=batch, seq=seq)
        y_attn = _attention(q, k, vt).reshape(m, MLA_WIDTH)
        y_ssd = _ssd(proj, small, conv_w[l], conv_b[l].reshape(1, -1), _pad_lanes(dt_bias[l]),
                     _pad_lanes(a_log[l]), jnp.repeat(d_skip[l], SSD_HEAD_DIM).reshape(1, -1),
                     ssd_norm[l].reshape(1, -1), batch=batch, seq=seq)
        xs = _outproj(y_attn, y_ssd, w_out_b, xs, l)
        xs = _ffn(xs, ffn2_norm[l], w2g, w2u, w2d, l, final_norm if l == depth - 1 else None)
    return xs.reshape(batch, seq, d)
```

```python
import functools
import math

import jax
import jax.numpy as jnp
from jax import lax
from jax.experimental import pallas as pl
from jax.experimental.pallas import tpu as pltpu

F32 = jnp.float32
BF16 = jnp.bfloat16

MLA_HEADS = 16
Q_LORA = 896
KV_LORA = 512
QK_NOPE = 128
QK_ROPE = 64
V_HEAD = 128
ROPE_THETA = 10000.0
SSD_HEAD_DIM = 64
SSD_HEADS = 32
SSD_GROUPS = 8
SSD_STATE = 128
SSD_CONV = 4
SSD_CHUNK = 128
SSD_WIDTH = SSD_HEADS * SSD_HEAD_DIM
MLA_WIDTH = MLA_HEADS * V_HEAD
EPS = 1e-6

LANES = 128
Q_LORA_PAD = 1024
SMALL_COLS = 512
QK_DIM = 256
PROJ_COLS = Q_LORA_PAD + KV_LORA + SMALL_COLS + SSD_WIDTH + (SSD_WIDTH + 2 * SSD_GROUPS * SSD_STATE)

ATTN_TILE = 512
Q_SCALE = (QK_NOPE + QK_ROPE) ** -0.5 * math.log2(math.e)

DOWN_CHUNK = 1024

NORM_ROWS = 128

VMEM_LIMIT = 60 * 1024 * 1024


def _params(sem):
    return pltpu.CompilerParams(dimension_semantics=sem, vmem_limit_bytes=VMEM_LIMIT)


def _rms(x, g):
    return x * lax.rsqrt(jnp.mean(x * x, axis=-1, keepdims=True) + EPS) * g


def _row_chunks(rows):
    step = min(rows, NORM_ROWS)
    return [slice(r, r + step) for r in range(0, rows, step)]


def _silu(x):
    return x * jax.nn.sigmoid(x)


def _ffn_kernel(x_ref, g_ref, wg_ref, wu_ref, wd_ref, *rest, final):
    if final:
        fn_ref, o_ref, h_ref = rest
    else:
        o_ref, h_ref = rest
    f = pl.program_id(1)

    @pl.when(f == 0)
    def _():
        for rows in _row_chunks(x_ref.shape[0]):
            x = x_ref[rows, :]
            h_ref[rows, :] = _rms(x, g_ref[...]).astype(BF16)
            o_ref[rows, :] = x

    h = h_ref[...]
    g = jnp.dot(h, wg_ref[...], preferred_element_type=F32)
    u = jnp.dot(h, wu_ref[...], preferred_element_type=F32)
    a = (0.5 * _silu(g) * u).astype(BF16)
    for c0 in range(0, o_ref.shape[1], DOWN_CHUNK):
        cols = slice(c0, c0 + DOWN_CHUNK)
        o_ref[:, cols] += jnp.dot(a, wd_ref[:, cols], preferred_element_type=F32)

    if final:
        @pl.when(f == pl.num_programs(1) - 1)
        def _():
            for rows in _row_chunks(o_ref.shape[0]):
                o_ref[rows, :] = _rms(o_ref[rows, :], fn_ref[...])


def _ffn(x, g, wg, wu, wd, layer, final_norm=None, *, tm=512, tf=512):
    m, d = x.shape
    ff = wg.shape[2]
    tm, tf = min(tm, m), min(tf, ff)
    final = final_norm is not None
    in_specs = [
        pl.BlockSpec((tm, d), lambda i, f: (i, 0), pipeline_mode=pl.Buffered(1)),
        pl.BlockSpec((1, d), lambda i, f: (0, 0)),
        pl.BlockSpec((None, d, tf), lambda i, f: (layer, 0, f)),
        pl.BlockSpec((None, d, tf), lambda i, f: (layer, 0, f)),
        pl.BlockSpec((None, tf, d), lambda i, f: (layer, f, 0)),
    ]
    args = [x, g.reshape(1, d), wg, wu, wd]
    if final:
        in_specs.append(pl.BlockSpec((1, d), lambda i, f: (0, 0)))
        args.append(final_norm.reshape(1, d))
    return pl.pallas_call(
        functools.partial(_ffn_kernel, final=final),
        grid=(m // tm, ff // tf),
        in_specs=in_specs,
        out_specs=pl.BlockSpec((tm, d), lambda i, f: (i, 0)),
        out_shape=jax.ShapeDtypeStruct((m, d), F32),
        scratch_shapes=[pltpu.VMEM((tm, d), BF16)],
        compiler_params=_params(("parallel", "arbitrary")),
        name="ffn",
    )(*args)


PROJ_TILE = 1024
SMALL_START = Q_LORA_PAD + KV_LORA


def _inproj_kernel(x_ref, g_ref, w_ref, o_ref, small_ref, h_ref):
    j = pl.program_id(1)

    @pl.when(j == 0)
    def _():
        for rows in _row_chunks(x_ref.shape[0]):
            h_ref[rows, :] = _rms(x_ref[rows, :], g_ref[...]).astype(BF16)

    y = jnp.dot(h_ref[...], w_ref[...], preferred_element_type=F32)
    o_ref[...] = y.astype(BF16)

    @pl.when(j == SMALL_START // PROJ_TILE)
    def _():
        lo = SMALL_START % PROJ_TILE
        small_ref[...] = y[:, lo:lo + SMALL_COLS]


def _inproj(x, g, w, layer, *, tm=512):
    m, d = x.shape
    n = w.shape[2]
    tm, tn = min(tm, m), PROJ_TILE
    return pl.pallas_call(
        _inproj_kernel,
        grid=(m // tm, n // tn),
        in_specs=[
            pl.BlockSpec((tm, d), lambda i, j: (i, 0)),
            pl.BlockSpec((1, d), lambda i, j: (0, 0)),
            pl.BlockSpec((None, d, tn), lambda i, j: (layer, 0, j)),
        ],
        out_specs=[
            pl.BlockSpec((tm, tn), lambda i, j: (i, j)),
            pl.BlockSpec((tm, SMALL_COLS), lambda i, j: (i, 0)),
        ],
        out_shape=[
            jax.ShapeDtypeStruct((m, n), BF16),
            jax.ShapeDtypeStruct((m, SMALL_COLS), F32),
        ],
        scratch_shapes=[pltpu.VMEM((tm, d), BF16)],
        compiler_params=_params(("parallel", "arbitrary")),
        name="inproj",
    )(x, g.reshape(1, d), w)


def _rope_fold(t):
    return t + pltpu.roll(t, QK_ROPE, axis=1)


def _qkv_kernel(cq_ref, ckv_ref, kpe_ref, pos_ref, freq_ref, qn_ref, kvn_ref, wq_ref, wkv_ref,
                q_ref, k_ref, vt_ref):
    hq = _rms(cq_ref[:, :Q_LORA].astype(F32), qn_ref[...]).astype(BF16)
    hkv = _rms(ckv_ref[...].astype(F32), kvn_ref[...]).astype(BF16)
    yq = jnp.dot(hq, wq_ref[...], preferred_element_type=F32)
    ykv = jnp.dot(hkv, wkv_ref[...], preferred_element_type=F32)

    ang = pos_ref[...].astype(F32) * freq_ref[...]
    lane = lax.broadcasted_iota(jnp.int32, ang.shape, 1)
    first_half = lane < QK_ROPE
    table = jnp.where(first_half, jnp.cos(ang), jnp.sin(ang))

    k_rope = jnp.where(first_half, _rope_fold(kpe_ref[...] * table), 0.0).astype(BF16)
    yq = yq * Q_SCALE
    nope_w = MLA_HEADS * QK_NOPE
    for h in range(MLA_HEADS):
        lo = h * LANES
        q_ref[0, h, :, :QK_NOPE] = yq[:, lo:lo + QK_NOPE].astype(BF16)
        q_ref[0, h, :, QK_NOPE:] = _rope_fold(yq[:, nope_w + lo:nope_w + lo + LANES] * table).astype(BF16)
        kv_lo = h * (QK_NOPE + V_HEAD)
        k_ref[0, h, :, :QK_NOPE] = ykv[:, kv_lo:kv_lo + QK_NOPE].astype(BF16)
        k_ref[0, h, :, QK_NOPE:] = k_rope
        vt_ref[0, h, 0] = ykv[:, kv_lo + QK_NOPE:kv_lo + QK_NOPE + V_HEAD].T.astype(BF16)


def _qkv(proj, small, pos, freq, qn, kvn, wq, wkv, layer, *, batch, seq):
    tm = min(ATTN_TILE, seq)
    nt = seq // tm
    row = lambda b, s: b * nt + s
    kv_blk = Q_LORA_PAD // KV_LORA
    hd = MLA_HEADS
    return pl.pallas_call(
        _qkv_kernel,
        grid=(batch, nt),
        in_specs=[
            pl.BlockSpec((tm, Q_LORA_PAD), lambda b, s: (row(b, s), 0)),
            pl.BlockSpec((tm, KV_LORA), lambda b, s: (row(b, s), kv_blk)),
            pl.BlockSpec((tm, LANES), lambda b, s: (row(b, s), 0)),
            pl.BlockSpec((tm, 1), lambda b, s: (row(b, s), 0)),
            pl.BlockSpec((1, LANES), lambda b, s: (0, 0)),
            pl.BlockSpec((1, Q_LORA), lambda b, s: (0, 0)),
            pl.BlockSpec((1, KV_LORA), lambda b, s: (0, 0)),
            pl.BlockSpec((None,) + wq.shape[1:], lambda b, s: (layer, 0, 0)),
            pl.BlockSpec((None,) + wkv.shape[1:], lambda b, s: (layer, 0, 0)),
        ],
        out_specs=[
            pl.BlockSpec((1, hd, tm, QK_DIM), lambda b, s: (b, 0, s, 0)),
            pl.BlockSpec((1, hd, tm, QK_DIM), lambda b, s: (b, 0, s, 0)),
            pl.BlockSpec((1, hd, 1, V_HEAD, tm), lambda b, s: (b, 0, s, 0, 0)),
        ],
        out_shape=[
            jax.ShapeDtypeStruct((batch, hd, seq, QK_DIM), BF16),
            jax.ShapeDtypeStruct((batch, hd, seq, QK_DIM), BF16),
            jax.ShapeDtypeStruct((batch, hd, nt, V_HEAD, tm), BF16),
        ],
        compiler_params=_params(("parallel", "parallel")),
        name="qkv_up",
    )(proj, proj, small, pos, freq, qn.reshape(1, -1), kvn.reshape(1, -1), wq, wkv)


def _attn_kernel(q_ref, k_ref, vt_ref, o_ref, s0_ref, s1_ref, m_ref, l_ref, acc_ref, *, nt, tq):
    s_refs = (s0_ref, s1_ref)
    pairs = [(qi, j) for qi in range(nt) for j in range(qi + 1)]

    def scores_into(s_ref, qi, j):
        q = q_ref[0, 0, qi * tq:(qi + 1) * tq, :]
        k = k_ref[0, 0, j * tq:(j + 1) * tq, :]
        s_ref[...] = lax.dot_general(k, q, (((1,), (1,)), ((), ())), preferred_element_type=F32)

    scores_into(s_refs[0], *pairs[0])
    for t, (qi, j) in enumerate(pairs):
        if t + 1 < len(pairs):
            scores_into(s_refs[(t + 1) % 2], *pairs[t + 1])
        s = s_refs[t % 2][...]
        if j == qi:
            key = lax.broadcasted_iota(jnp.int32, s.shape, 0)
            query = lax.broadcasted_iota(jnp.int32, s.shape, 1)
            s = jnp.where(query >= key, s, -jnp.inf)
        m_tile = jnp.max(s, axis=0, keepdims=True)
        if j == 0:
            m_new = m_tile
        else:
            m_old = m_ref[...]
            m_new = jnp.maximum(m_old, m_tile)
            alpha = jnp.exp2(m_old - m_new)
        p = jnp.exp2(s - m_new)
        l_tile = jnp.sum(p, axis=0, keepdims=True)
        pv = jnp.dot(vt_ref[0, 0, j], p.astype(BF16), preferred_element_type=F32)
        if j == 0:
            l_ref[...] = l_tile
            acc_ref[...] = pv
        else:
            l_ref[...] = alpha * l_ref[...] + l_tile
            acc_ref[...] = alpha * acc_ref[...] + pv
        m_ref[...] = m_new
        if j == qi:
            o_ref[0, qi * tq:(qi + 1) * tq, :] = (acc_ref[...] / l_ref[...]).T.astype(o_ref.dtype)


def _attention(q, k, vt):
    b, hd, s, _ = q.shape
    nt, tq = vt.shape[2], vt.shape[4]
    return pl.pallas_call(
        functools.partial(_attn_kernel, nt=nt, tq=tq),
        grid=(b, hd),
        in_specs=[
            pl.BlockSpec((1, 1, s, QK_DIM), lambda bi, h: (bi, h, 0, 0)),
            pl.BlockSpec((1, 1, s, QK_DIM), lambda bi, h: (bi, h, 0, 0)),
            pl.BlockSpec((1, 1, nt, V_HEAD, tq), lambda bi, h: (bi, h, 0, 0, 0)),
        ],
        out_specs=pl.BlockSpec((1, s, V_HEAD), lambda bi, h: (bi, 0, h)),
        out_shape=jax.ShapeDtypeStruct((b, s, hd * V_HEAD), BF16),
        scratch_shapes=[
            pltpu.VMEM((tq, tq), F32),
            pltpu.VMEM((tq, tq), F32),
            pltpu.VMEM((1, tq), F32),
            pltpu.VMEM((1, tq), F32),
            pltpu.VMEM((V_HEAD, tq), F32),
        ],
        compiler_params=_params(("parallel", "parallel")),
        name="mla_attention",
    )(q, k, vt)


CONV_HALO = 8
HEADS_PER_GROUP = SSD_HEADS // SSD_GROUPS
GROUP_WIDTH = HEADS_PER_GROUP * SSD_HEAD_DIM


def _ssd_kernel(xbc_ref, z_ref, dt_ref, cw_ref, cb_ref, dtb_ref, alog_ref, dskip_ref, nw_ref,
                o_ref, ext_ref, st_ref):
    L = SSD_CHUNK
    c = pl.program_id(1)

    @pl.when(c == 0)
    def _():
        ext_ref[0:CONV_HALO, :] = jnp.zeros((CONV_HALO, ext_ref.shape[1]), F32)
        st_ref[...] = jnp.zeros(st_ref.shape, F32)

    @pl.when(c > 0)
    def _():
        ext_ref[0:CONV_HALO, :] = ext_ref[L:L + CONV_HALO, :]

    ext_ref[CONV_HALO:CONV_HALO + L, :] = xbc_ref[...].astype(F32)
    conv = cb_ref[...]
    for t in range(SSD_CONV):
        off = CONV_HALO - (SSD_CONV - 1) + t
        conv = conv + cw_ref[t:t + 1, :] * ext_ref[off:off + L, :]
    u = _silu(conv)

    stats = _ssd_decays(dt_ref, dtb_ref, alog_ref)
    tril = (lax.broadcasted_iota(jnp.int32, (L, L), 0) >= lax.broadcasted_iota(jnp.int32, (L, L), 1))
    lo_half = lax.broadcasted_iota(jnp.int32, (L, LANES), 1) < SSD_HEAD_DIM
    lo_half_row = lax.broadcasted_iota(jnp.int32, (1, LANES), 1) < SSD_HEAD_DIM
    for g in range(SSD_GROUPS):
        _ssd_group(g, stats, (tril, lo_half, lo_half_row), u, z_ref, dskip_ref, nw_ref, o_ref, st_ref)


def _ssd_decays(dt_ref, dtb_ref, alog_ref):
    L = SSD_CHUNK
    dt = jax.nn.softplus(dt_ref[...] + dtb_ref[...])
    a = dt * (-jnp.exp(alog_ref[...]))
    row = lax.broadcasted_iota(jnp.int32, (L, L), 0)
    col = lax.broadcasted_iota(jnp.int32, (L, L), 1)
    a_cs = jnp.dot((row >= col).astype(F32), a, preferred_element_type=F32,
                   precision=lax.Precision.HIGHEST)
    a_cs_t = jnp.dot(a.T, (row <= col).astype(F32), preferred_element_type=F32,
                     precision=lax.Precision.HIGHEST)
    a_end = a_cs[L - 1:L, :]
    e_cs = jnp.exp(a_cs)
    e_end = jnp.exp(a_end - a_cs)
    return dt, a_cs, a_cs_t, a_end, e_cs, e_end


def _ssd_group(g, stats, masks, u, z_ref, dskip_ref, nw_ref, o_ref, st_ref):
    dt, a_cs, a_cs_t, a_end, e_cs, e_end = stats
    tril, lo_half, lo_half_row = masks
    b_off = SSD_WIDTH
    c_off = SSD_WIDTH + SSD_GROUPS * SSD_STATE

    def pair_cols(v, h0):
        return jnp.where(lo_half, v[:, h0:h0 + 1], v[:, h0 + 1:h0 + 2])

    bg = u[:, b_off + g * SSD_STATE:b_off + (g + 1) * SSD_STATE].astype(BF16)
    cg = u[:, c_off + g * SSD_STATE:c_off + (g + 1) * SSD_STATE].astype(BF16)
    cb = lax.dot_general(cg, bg, (((1,), (1,)), ((), ())), preferred_element_type=F32)
    st = st_ref[g]
    y_off = jnp.dot(cg, st.astype(BF16), preferred_element_type=F32)
    ys, xscs, cds = [], [], []
    for pr in range(HEADS_PER_GROUP // 2):
        h0 = g * HEADS_PER_GROUP + 2 * pr
        x0 = h0 * SSD_HEAD_DIM
        xp = u[:, x0:x0 + LANES]
        xdt = xp * pair_cols(dt, h0)
        ms = []
        for h in (h0, h0 + 1):
            diff = a_cs[:, h:h + 1] - a_cs_t[h:h + 1, :]
            ms.append((cb * jnp.exp(jnp.where(tril, diff, -jnp.inf))).astype(BF16))
        m2 = jnp.concatenate(ms, axis=1)
        xd = jnp.concatenate([jnp.where(lo_half, xdt, 0.0), jnp.where(lo_half, 0.0, xdt)],
                             axis=0).astype(BF16)
        y_diag = jnp.dot(m2, xd, preferred_element_type=F32)
        yo = y_off[:, 2 * pr * SSD_HEAD_DIM:2 * pr * SSD_HEAD_DIM + LANES]
        ys.append(y_diag + pair_cols(e_cs, h0) * yo + dskip_ref[:, x0:x0 + LANES] * xp)
        xscs.append((xdt * pair_cols(e_end, h0)).astype(BF16))
        cds.append(jnp.where(lo_half_row, a_end[:, h0:h0 + 1], a_end[:, h0 + 1:h0 + 2]))
    xsc = jnp.concatenate(xscs, axis=1)
    new = lax.dot_general(bg, xsc, (((0,), (0,)), ((), ())), preferred_element_type=F32)
    st_ref[g] = st * jnp.exp(jnp.concatenate(cds, axis=1)) + new

    gw = slice(g * GROUP_WIDTH, (g + 1) * GROUP_WIDTH)
    yg = jnp.concatenate(ys, axis=1) * _silu(z_ref[:, gw].astype(F32))
    yg = yg * lax.rsqrt(jnp.mean(yg * yg, axis=-1, keepdims=True) + EPS)
    o_ref[:, gw] = (yg * nw_ref[:, gw]).astype(o_ref.dtype)


def _ssd(proj, small, cw, cb, dtb, alog, dskip, nw, *, batch, seq):
    L = SSD_CHUNK
    nc = seq // L
    conv_dim = cw.shape[1]
    row = lambda b, c: b * nc + c
    xbc_blk = (PROJ_COLS - conv_dim) // conv_dim
    z_blk = (PROJ_COLS - conv_dim - SSD_WIDTH) // SSD_WIDTH
    dt_blk = 1
    const = lambda b, c: (0, 0)
    return pl.pallas_call(
        _ssd_kernel,
        grid=(batch, nc),
        in_specs=[
            pl.BlockSpec((L, conv_dim), lambda b, c: (row(b, c), xbc_blk)),
            pl.BlockSpec((L, SSD_WIDTH), lambda b, c: (row(b, c), z_blk)),
            pl.BlockSpec((L, LANES), lambda b, c: (row(b, c), dt_blk)),
            pl.BlockSpec((SSD_CONV, conv_dim), const),
            pl.BlockSpec((1, conv_dim), const),
            pl.BlockSpec((1, LANES), const),
            pl.BlockSpec((1, LANES), const),
            pl.BlockSpec((1, SSD_WIDTH), const),
            pl.BlockSpec((1, SSD_WIDTH), const),
        ],
        out_specs=pl.BlockSpec((L, SSD_WIDTH), lambda b, c: (row(b, c), 0)),
        out_shape=jax.ShapeDtypeStruct((batch * seq, SSD_WIDTH), BF16),
        scratch_shapes=[
            pltpu.VMEM((CONV_HALO + L, conv_dim), F32),
            pltpu.VMEM((SSD_GROUPS, SSD_STATE, GROUP_WIDTH), F32),
        ],
        compiler_params=_params(("parallel", "arbitrary")),
        name="ssd",
    )(proj, proj, small, cw, cb, dtb, alog, dskip, nw)


def _outproj_kernel(ya_ref, ys_ref, wa_ref, ws_ref, x_ref, o_ref):
    o_ref[...] = (x_ref[...]
                  + jnp.dot(ya_ref[...], wa_ref[...], preferred_element_type=F32)
                  + jnp.dot(ys_ref[...], ws_ref[...], preferred_element_type=F32))


def _outproj(ya, ys, w, x, layer, *, tm=1024, tn=512):
    m, d = x.shape
    ka, ks = ya.shape[1], ys.shape[1]
    assert ka == ks
    tm, tn = min(tm, m), min(tn, d)
    return pl.pallas_call(
        _outproj_kernel,
        grid=(m // tm, d // tn),
        in_specs=[
            pl.BlockSpec((tm, ka), lambda i, j: (i, 0)),
            pl.BlockSpec((tm, ks), lambda i, j: (i, 0)),
            pl.BlockSpec((None, ka, tn), lambda i, j: (layer, 0, j)),
            pl.BlockSpec((None, ks, tn), lambda i, j: (layer, 1, j)),
            pl.BlockSpec((tm, tn), lambda i, j: (i, j)),
        ],
        out_specs=pl.BlockSpec((tm, tn), lambda i, j: (i, j)),
        out_shape=jax.ShapeDtypeStruct((m, d), F32),
        compiler_params=_params(("parallel", "arbitrary")),
        name="outproj",
    )(ya, ys, w, w, x)


def _rotate_half_cols(w):
    half = w.shape[-1] // 2
    return jnp.concatenate([-w[..., half:], w[..., :half]], axis=-1)


W_IN_TILE = 256
XPOSE_ROWS = 512


def _relayout_w_in_kernel(wt_ref, small_t_ref, o_ref):
    o1 = Q_LORA
    o2 = o1 + KV_LORA
    o3 = o2 + QK_ROPE
    o4 = o3 + SSD_WIDTH
    z_start = SMALL_START + SMALL_COLS

    def put(dst, src, n):
        for r in range(0, n, XPOSE_ROWS):
            rows = min(XPOSE_ROWS, n - r)
            o_ref[:, dst + r:dst + r + rows] = wt_ref[src + r:src + r + rows, :].T.astype(BF16)

    put(0, 0, o1)
    o_ref[:, o1:Q_LORA_PAD] = jnp.zeros((o_ref.shape[0], Q_LORA_PAD - o1), BF16)
    put(Q_LORA_PAD, o1, KV_LORA)
    o_ref[:, SMALL_START:z_start] = small_t_ref[...].T.astype(BF16)
    put(z_start, o3, SSD_WIDTH)
    put(z_start + SSD_WIDTH, o4, o_ref.shape[1] - z_start - SSD_WIDTH)


def _prep_w_in(w_in):
    depth, d, cols = w_in.shape
    wt = jnp.swapaxes(w_in, 1, 2)
    o2 = Q_LORA + KV_LORA
    o3 = o2 + QK_ROPE
    kpe = wt[:, o2:o3]
    half = QK_ROPE // 2
    small_t = jnp.concatenate(
        [kpe, -kpe[:, half:], kpe[:, :half], wt[:, cols - SSD_HEADS:],
         jnp.zeros((depth, SMALL_COLS - 2 * QK_ROPE - SSD_HEADS, d), w_in.dtype)], axis=1)
    tk = min(W_IN_TILE, d)
    return pl.pallas_call(
        _relayout_w_in_kernel,
        grid=(depth, d // tk),
        in_specs=[
            pl.BlockSpec((None, cols, tk), lambda l, r: (l, 0, r)),
            pl.BlockSpec((None, SMALL_COLS, tk), lambda l, r: (l, 0, r)),
        ],
        out_specs=pl.BlockSpec((None, tk, PROJ_COLS), lambda l, r: (l, r, 0)),
        out_shape=jax.ShapeDtypeStruct((depth, d, PROJ_COLS), BF16),
        compiler_params=_params(("parallel", "parallel")),
        name="relayout_w_in",
    )(wt, small_t)


def _prep_w_q(w):
    lead = w.shape[:-1]
    w = w.reshape(lead + (MLA_HEADS, QK_NOPE + QK_ROPE))
    rope = w[..., QK_NOPE:]
    rope = jnp.concatenate([rope, _rotate_half_cols(rope)], axis=-1)
    return jnp.concatenate([w[..., :QK_NOPE].reshape(lead + (-1,)), rope.reshape(lead + (-1,))],
                           axis=-1).astype(BF16)


def _pad_lanes(v):
    return jnp.pad(v, (0, LANES - v.shape[0])).reshape(1, LANES)


def kernel(x, positions, ffn1_norm, ffn1_w_gate, ffn1_w_up, ffn1_w_down, mix_norm, w_in, q_a_norm, w_q_up, kv_a_norm, w_kv_up, conv_w, conv_b, dt_bias, a_log, d_skip, ssd_norm, w_out, ffn2_norm, ffn2_w_gate, ffn2_w_up, ffn2_w_down, final_norm):
    batch, seq, d = x.shape
    depth = w_in.shape[0]
    m = batch * seq
    xs = x.reshape(m, d)
    pos = positions.reshape(m, 1)
    inv_freq = ROPE_THETA ** (-jnp.arange(0, QK_ROPE, 2, dtype=F32) / QK_ROPE)
    freq = jnp.tile(inv_freq, LANES // inv_freq.shape[0]).reshape(1, LANES)

    w1g, w1u, w1d = ffn1_w_gate.astype(BF16), ffn1_w_up.astype(BF16), ffn1_w_down.astype(BF16)
    w2g, w2u, w2d = ffn2_w_gate.astype(BF16), ffn2_w_up.astype(BF16), ffn2_w_down.astype(BF16)
    w_in_b, w_q_b, w_kv_b = _prep_w_in(w_in), _prep_w_q(w_q_up), w_kv_up.astype(BF16)
    w_out_b = w_out.astype(BF16)

    for l in range(depth):
        xs = _ffn(xs, ffn1_norm[l], w1g, w1u, w1d, l)
        proj, small = _inproj(xs, mix_norm[l], w_in_b, l)
        q, k, vt = _qkv(proj, small, pos, freq, q_a_norm[l], kv_a_norm[l], w_q_b, w_kv_b, l,
                        batch=batch, seq=seq)
        y_attn = _attention(q, k, vt).reshape(m, MLA_WIDTH)
        y_ssd = _ssd(proj, small, conv_w[l], conv_b[l].reshape(1, -1), _pad_lanes(dt_bias[l]),
                     _pad_lanes(a_log[l]), jnp.repeat(d_skip[l], SSD_HEAD_DIM).reshape(1, -1),
                     ssd_norm[l].reshape(1, -1), batch=batch, seq=seq)
        xs = _outproj(y_attn, y_ssd, w_out_b, xs, l)
        xs = _ffn(xs, ffn2_norm[l], w2g, w2u, w2d, l, final_norm if l == depth - 1 else None)
    return xs.reshape(batch, seq, d)
```

```python
import functools
import math

import jax
import jax.numpy as jnp
from jax import lax
from jax.experimental import pallas as pl
from jax.experimental.pallas import tpu as pltpu

F32 = jnp.float32
BF16 = jnp.bfloat16

MLA_HEADS = 16
Q_LORA = 896
KV_LORA = 512
QK_NOPE = 128
QK_ROPE = 64
V_HEAD = 128
ROPE_THETA = 10000.0
SSD_HEAD_DIM = 64
SSD_HEADS = 32
SSD_GROUPS = 8
SSD_STATE = 128
SSD_CONV = 4
SSD_CHUNK = 128
SSD_WIDTH = SSD_HEADS * SSD_HEAD_DIM
MLA_WIDTH = MLA_HEADS * V_HEAD
EPS = 1e-6

LANES = 128
Q_LORA_PAD = 1024
SMALL_COLS = 512
QK_DIM = 256
PROJ_COLS = Q_LORA_PAD + KV_LORA + SMALL_COLS + SSD_WIDTH + (SSD_WIDTH + 2 * SSD_GROUPS * SSD_STATE)

ATTN_TILE = 512
Q_SCALE = (QK_NOPE + QK_ROPE) ** -0.5 * math.log2(math.e)

DOWN_CHUNK = 1024

NORM_ROWS = 128

VMEM_LIMIT = 60 * 1024 * 1024


def _params(sem):
    return pltpu.CompilerParams(dimension_semantics=sem, vmem_limit_bytes=VMEM_LIMIT)


def _rms(x, g):
    return x * lax.rsqrt(jnp.mean(x * x, axis=-1, keepdims=True) + EPS) * g


def _row_chunks(rows):
    step = min(rows, NORM_ROWS)
    return [slice(r, r + step) for r in range(0, rows, step)]


def _silu(x):
    return x * jax.nn.sigmoid(x)


def _ffn_kernel(x_hbm, g_ref, wg_ref, wu_ref, wd_ref, *rest, final):
    if final:
        fn_ref, o_ref, h_ref, x_ref, x_sem = rest
    else:
        o_ref, h_ref, x_ref, x_sem = rest
    i = pl.program_id(0)
    f = pl.program_id(1)
    tm = x_ref.shape[0]

    def x_copy(tile):
        rows = pl.ds(pl.multiple_of(tile * tm, tm), tm)
        return pltpu.make_async_copy(x_hbm.at[rows, :], x_ref, x_sem)

    @pl.when(f == 0)
    def _():
        @pl.when(i == 0)
        def _():
            x_copy(0).start()

        x_copy(i).wait()
        for rows in _row_chunks(tm):
            x = x_ref[rows, :]
            h_ref[rows, :] = _rms(x, g_ref[...]).astype(BF16)
            o_ref[rows, :] = x

    @pl.when((f == 1) & (i + 1 < pl.num_programs(0)))
    def _():
        x_copy(i + 1).start()

    h = h_ref[...]
    g = jnp.dot(h, wg_ref[...], preferred_element_type=F32)
    u = jnp.dot(h, wu_ref[...], preferred_element_type=F32)
    a = (0.5 * _silu(g) * u).astype(BF16)
    for c0 in range(0, o_ref.shape[1], DOWN_CHUNK):
        cols = slice(c0, c0 + DOWN_CHUNK)
        o_ref[:, cols] += jnp.dot(a, wd_ref[:, cols], preferred_element_type=F32)

    if final:
        @pl.when(f == pl.num_programs(1) - 1)
        def _():
            for rows in _row_chunks(o_ref.shape[0]):
                o_ref[rows, :] = _rms(o_ref[rows, :], fn_ref[...])


def _ffn(x, g, wg, wu, wd, layer, final_norm=None, *, tm=512, tf=512):
    m, d = x.shape
    ff = wg.shape[2]
    tm, tf = min(tm, m), min(tf, ff)
    final = final_norm is not None
    in_specs = [
        pl.BlockSpec(memory_space=pl.ANY),
        pl.BlockSpec((1, d), lambda i, f: (0, 0)),
        pl.BlockSpec((None, d, tf), lambda i, f: (layer, 0, f)),
        pl.BlockSpec((None, d, tf), lambda i, f: (layer, 0, f)),
        pl.BlockSpec((None, tf, d), lambda i, f: (layer, f, 0)),
    ]
    args = [x, g.reshape(1, d), wg, wu, wd]
    if final:
        in_specs.append(pl.BlockSpec((1, d), lambda i, f: (0, 0)))
        args.append(final_norm.reshape(1, d))
    return pl.pallas_call(
        functools.partial(_ffn_kernel, final=final),
        grid=(m // tm, ff // tf),
        in_specs=in_specs,
        out_specs=pl.BlockSpec((tm, d), lambda i, f: (i, 0)),
        out_shape=jax.ShapeDtypeStruct((m, d), F32),
        scratch_shapes=[
            pltpu.VMEM((tm, d), BF16),
            pltpu.VMEM((tm, d), F32),
            pltpu.SemaphoreType.DMA(()),
        ],
        compiler_params=_params(("arbitrary", "arbitrary")),
        name="ffn",
    )(*args)


PROJ_TILE = 1024
SMALL_START = Q_LORA_PAD + KV_LORA


def _inproj_kernel(x_ref, g_ref, w_ref, o_ref, small_ref, h_ref):
    j = pl.program_id(1)

    @pl.when(j == 0)
    def _():
        for rows in _row_chunks(x_ref.shape[0]):
            h_ref[rows, :] = _rms(x_ref[rows, :], g_ref[...]).astype(BF16)

    y = jnp.dot(h_ref[...], w_ref[...], preferred_element_type=F32)
    o_ref[...] = y.astype(BF16)

    @pl.when(j == SMALL_START // PROJ_TILE)
    def _():
        lo = SMALL_START % PROJ_TILE
        small_ref[...] = y[:, lo:lo + SMALL_COLS]


def _inproj(x, g, w, layer, *, tm=512):
    m, d = x.shape
    n = w.shape[2]
    tm, tn = min(tm, m), PROJ_TILE
    return pl.pallas_call(
        _inproj_kernel,
        grid=(m // tm, n // tn),
        in_specs=[
            pl.BlockSpec((tm, d), lambda i, j: (i, 0)),
            pl.BlockSpec((1, d), lambda i, j: (0, 0)),
            pl.BlockSpec((None, d, tn), lambda i, j: (layer, 0, j)),
        ],
        out_specs=[
            pl.BlockSpec((tm, tn), lambda i, j: (i, j)),
            pl.BlockSpec((tm, SMALL_COLS), lambda i, j: (i, 0)),
        ],
        out_shape=[
            jax.ShapeDtypeStruct((m, n), BF16),
            jax.ShapeDtypeStruct((m, SMALL_COLS), F32),
        ],
        scratch_shapes=[pltpu.VMEM((tm, d), BF16)],
        compiler_params=_params(("parallel", "arbitrary")),
        name="inproj",
    )(x, g.reshape(1, d), w)


def _rope_fold(t):
    return t + pltpu.roll(t, QK_ROPE, axis=1)


def _qkv_kernel(cq_ref, ckv_ref, kpe_ref, pos_ref, freq_ref, qn_ref, kvn_ref, wq_ref, wkv_ref,
                qt_ref, k_ref, vt_ref):
    hq = _rms(cq_ref[:, :Q_LORA].astype(F32), qn_ref[...]).astype(BF16)
    hkv = _rms(ckv_ref[...].astype(F32), kvn_ref[...]).astype(BF16)
    yq = jnp.dot(hq, wq_ref[...], preferred_element_type=F32)
    ykv = jnp.dot(hkv, wkv_ref[...], preferred_element_type=F32)

    ang = pos_ref[...].astype(F32) * freq_ref[...]
    lane = lax.broadcasted_iota(jnp.int32, ang.shape, 1)
    first_half = lane < QK_ROPE
    table = jnp.where(first_half, jnp.cos(ang), jnp.sin(ang))

    k_rope = jnp.where(first_half, _rope_fold(kpe_ref[...] * table), 0.0).astype(BF16)
    yq = yq * Q_SCALE
    nope_w = MLA_HEADS * QK_NOPE
    for h in range(MLA_HEADS):
        lo = h * LANES
        qt_ref[0, h, 0, :QK_NOPE, :] = yq[:, lo:lo + QK_NOPE].T.astype(BF16)
        qt_ref[0, h, 0, QK_NOPE:, :] = _rope_fold(yq[:, nope_w + lo:nope_w + lo + LANES] * table).T.astype(BF16)
        kv_lo = h * (QK_NOPE + V_HEAD)
        k_ref[0, h, :, :QK_NOPE] = ykv[:, kv_lo:kv_lo + QK_NOPE].astype(BF16)
        k_ref[0, h, :, QK_NOPE:] = k_rope
        vt_ref[0, h, 0] = ykv[:, kv_lo + QK_NOPE:kv_lo + QK_NOPE + V_HEAD].T.astype(BF16)


def _qkv(proj, small, pos, freq, qn, kvn, wq, wkv, layer, *, batch, seq):
    tm = min(ATTN_TILE, seq)
    nt = seq // tm
    row = lambda b, s: b * nt + s
    kv_blk = Q_LORA_PAD // KV_LORA
    hd = MLA_HEADS
    return pl.pallas_call(
        _qkv_kernel,
        grid=(batch, nt),
        in_specs=[
            pl.BlockSpec((tm, Q_LORA_PAD), lambda b, s: (row(b, s), 0)),
            pl.BlockSpec((tm, KV_LORA), lambda b, s: (row(b, s), kv_blk)),
            pl.BlockSpec((tm, LANES), lambda b, s: (row(b, s), 0)),
            pl.BlockSpec((tm, 1), lambda b, s: (row(b, s), 0)),
            pl.BlockSpec((1, LANES), lambda b, s: (0, 0)),
            pl.BlockSpec((1, Q_LORA), lambda b, s: (0, 0)),
            pl.BlockSpec((1, KV_LORA), lambda b, s: (0, 0)),
            pl.BlockSpec((None,) + wq.shape[1:], lambda b, s: (layer, 0, 0)),
            pl.BlockSpec((None,) + wkv.shape[1:], lambda b, s: (layer, 0, 0)),
        ],
        out_specs=[
            pl.BlockSpec((1, hd, 1, QK_DIM, tm), lambda b, s: (b, 0, s, 0, 0)),
            pl.BlockSpec((1, hd, tm, QK_DIM), lambda b, s: (b, 0, s, 0)),
            pl.BlockSpec((1, hd, 1, V_HEAD, tm), lambda b, s: (b, 0, s, 0, 0)),
        ],
        out_shape=[
            jax.ShapeDtypeStruct((batch, hd, nt, QK_DIM, tm), BF16),
            jax.ShapeDtypeStruct((batch, hd, seq, QK_DIM), BF16),
            jax.ShapeDtypeStruct((batch, hd, nt, V_HEAD, tm), BF16),
        ],
        compiler_params=_params(("parallel", "parallel")),
        name="qkv_up",
    )(proj, proj, small, pos, freq, qn.reshape(1, -1), kvn.reshape(1, -1), wq, wkv)


def _attn_kernel(qt_ref, k_ref, vt_ref, o_ref, s0_ref, s1_ref, m_ref, l_ref, acc_ref, *, nt, tq):
    s_refs = (s0_ref, s1_ref)
    pairs = [(qi, j) for qi in range(nt) for j in range(qi + 1)]

    def scores_into(s_ref, qi, j):
        k = k_ref[0, 0, j * tq:(j + 1) * tq, :]
        s_ref[...] = jnp.dot(k, qt_ref[0, 0, qi], preferred_element_type=F32)

    scores_into(s_refs[0], *pairs[0])
    for t, (qi, j) in enumerate(pairs):
        if t + 1 < len(pairs):
            scores_into(s_refs[(t + 1) % 2], *pairs[t + 1])
        s = s_refs[t % 2][...]
        if j == qi:
            key = lax.broadcasted_iota(jnp.int32, s.shape, 0)
            query = lax.broadcasted_iota(jnp.int32, s.shape, 1)
            s = jnp.where(query >= key, s, -jnp.inf)
        m_tile = jnp.max(s, axis=0, keepdims=True)
        if j == 0:
            m_new = m_tile
        else:
            m_old = m_ref[...]
            m_new = jnp.maximum(m_old, m_tile)
            alpha = jnp.exp2(m_old - m_new)
        p = jnp.exp2(s - m_new)
        l_tile = jnp.sum(p, axis=0, keepdims=True)
        pv = jnp.dot(vt_ref[0, 0, j], p.astype(BF16), preferred_element_type=F32)
        if j == 0:
            l_ref[...] = l_tile
            acc_ref[...] = pv
        else:
            l_ref[...] = alpha * l_ref[...] + l_tile
            acc_ref[...] = alpha * acc_ref[...] + pv
        m_ref[...] = m_new
        if j == qi:
            o_ref[0, qi * tq:(qi + 1) * tq, :] = (acc_ref[...] / l_ref[...]).T.astype(o_ref.dtype)


def _attention(qt, k, vt):
    b, hd, s, _ = k.shape
    nt, tq = vt.shape[2], vt.shape[4]
    return pl.pallas_call(
        functools.partial(_attn_kernel, nt=nt, tq=tq),
        grid=(b, hd),
        in_specs=[
            pl.BlockSpec((1, 1, nt, QK_DIM, tq), lambda bi, h: (bi, h, 0, 0, 0)),
            pl.BlockSpec((1, 1, s, QK_DIM), lambda bi, h: (bi, h, 0, 0)),
            pl.BlockSpec((1, 1, nt, V_HEAD, tq), lambda bi, h: (bi, h, 0, 0, 0)),
        ],
        out_specs=pl.BlockSpec((1, s, V_HEAD), lambda bi, h: (bi, 0, h)),
        out_shape=jax.ShapeDtypeStruct((b, s, hd * V_HEAD), BF16),
        scratch_shapes=[
            pltpu.VMEM((tq, tq), F32),
            pltpu.VMEM((tq, tq), F32),
            pltpu.VMEM((1, tq), F32),
            pltpu.VMEM((1, tq), F32),
            pltpu.VMEM((V_HEAD, tq), F32),
        ],
        compiler_params=_params(("parallel", "parallel")),
        name="mla_attention",
    )(qt, k, vt)


CONV_HALO = 8
HEADS_PER_GROUP = SSD_HEADS // SSD_GROUPS
GROUP_WIDTH = HEADS_PER_GROUP * SSD_HEAD_DIM


def _ssd_kernel(xbc_ref, z_ref, dt_ref, cw_ref, cb_ref, dtb_ref, alog_ref, dskip_ref, nw_ref,
                o_ref, ext_ref, st_ref):
    L = SSD_CHUNK
    c = pl.program_id(1)

    @pl.when(c == 0)
    def _():
        ext_ref[0:CONV_HALO, :] = jnp.zeros((CONV_HALO, ext_ref.shape[1]), F32)
        st_ref[...] = jnp.zeros(st_ref.shape, F32)

    @pl.when(c > 0)
    def _():
        ext_ref[0:CONV_HALO, :] = ext_ref[L:L + CONV_HALO, :]

    ext_ref[CONV_HALO:CONV_HALO + L, :] = xbc_ref[...].astype(F32)
    conv = cb_ref[...]
    for t in range(SSD_CONV):
        off = CONV_HALO - (SSD_CONV - 1) + t
        conv = conv + cw_ref[t:t + 1, :] * ext_ref[off:off + L, :]
    u = _silu(conv)

    stats = _ssd_decays(dt_ref, dtb_ref, alog_ref)
    tril = (lax.broadcasted_iota(jnp.int32, (L, L), 0) >= lax.broadcasted_iota(jnp.int32, (L, L), 1))
    lo_half = lax.broadcasted_iota(jnp.int32, (L, LANES), 1) < SSD_HEAD_DIM
    lo_half_row = lax.broadcasted_iota(jnp.int32, (1, LANES), 1) < SSD_HEAD_DIM
    for g in range(SSD_GROUPS):
        _ssd_group(g, stats, (tril, lo_half, lo_half_row), u, z_ref, dskip_ref, nw_ref, o_ref, st_ref)


def _ssd_decays(dt_ref, dtb_ref, alog_ref):
    L = SSD_CHUNK
    dt = jax.nn.softplus(dt_ref[...] + dtb_ref[...])
    a = dt * (-jnp.exp(alog_ref[...]))
    row = lax.broadcasted_iota(jnp.int32, (L, L), 0)
    col = lax.broadcasted_iota(jnp.int32, (L, L), 1)
    a_cs = jnp.dot((row >= col).astype(F32), a, preferred_element_type=F32,
                   precision=lax.Precision.HIGHEST)
    a_cs_t = jnp.dot(a.T, (row <= col).astype(F32), preferred_element_type=F32,
                     precision=lax.Precision.HIGHEST)
    a_end = a_cs[L - 1:L, :]
    e_cs = jnp.exp(a_cs)
    e_end = jnp.exp(a_end - a_cs)
    return dt, a_cs, a_cs_t, a_end, e_cs, e_end


def _ssd_group(g, stats, masks, u, z_ref, dskip_ref, nw_ref, o_ref, st_ref):
    dt, a_cs, a_cs_t, a_end, e_cs, e_end = stats
    tril, lo_half, lo_half_row = masks
    b_off = SSD_WIDTH
    c_off = SSD_WIDTH + SSD_GROUPS * SSD_STATE

    def pair_cols(v, h0):
        return jnp.where(lo_half, v[:, h0:h0 + 1], v[:, h0 + 1:h0 + 2])

    bg = u[:, b_off + g * SSD_STATE:b_off + (g + 1) * SSD_STATE].astype(BF16)
    cg = u[:, c_off + g * SSD_STATE:c_off + (g + 1) * SSD_STATE].astype(BF16)
    cb = lax.dot_general(cg, bg, (((1,), (1,)), ((), ())), preferred_element_type=F32)
    st = st_ref[g]
    y_off = jnp.dot(cg, st.astype(BF16), preferred_element_type=F32)
    ys, xscs, cds = [], [], []
    for pr in range(HEADS_PER_GROUP // 2):
        h0 = g * HEADS_PER_GROUP + 2 * pr
        x0 = h0 * SSD_HEAD_DIM
        xp = u[:, x0:x0 + LANES]
        xdt = xp * pair_cols(dt, h0)
        ms = []
        for h in (h0, h0 + 1):
            diff = a_cs[:, h:h + 1] - a_cs_t[h:h + 1, :]
            ms.append((cb * jnp.exp(jnp.where(tril, diff, -jnp.inf))).astype(BF16))
        m2 = jnp.concatenate(ms, axis=1)
        xd = jnp.concatenate([jnp.where(lo_half, xdt, 0.0), jnp.where(lo_half, 0.0, xdt)],
                             axis=0).astype(BF16)
        y_diag = jnp.dot(m2, xd, preferred_element_type=F32)
        yo = y_off[:, 2 * pr * SSD_HEAD_DIM:2 * pr * SSD_HEAD_DIM + LANES]
        ys.append(y_diag + pair_cols(e_cs, h0) * yo + dskip_ref[:, x0:x0 + LANES] * xp)
        xscs.append((xdt * pair_cols(e_end, h0)).astype(BF16))
        cds.append(jnp.where(lo_half_row, a_end[:, h0:h0 + 1], a_end[:, h0 + 1:h0 + 2]))
    xsc = jnp.concatenate(xscs, axis=1)
    new = lax.dot_general(bg, xsc, (((0,), (0,)), ((), ())), preferred_element_type=F32)
    st_ref[g] = st * jnp.exp(jnp.concatenate(cds, axis=1)) + new

    gw = slice(g * GROUP_WIDTH, (g + 1) * GROUP_WIDTH)
    yg = jnp.concatenate(ys, axis=1) * _silu(z_ref[:, gw].astype(F32))
    yg = yg * lax.rsqrt(jnp.mean(yg * yg, axis=-1, keepdims=True) + EPS)
    o_ref[:, gw] = (yg * nw_ref[:, gw]).astype(o_ref.dtype)


def _ssd(proj, small, cw, cb, dtb, alog, dskip, nw, *, batch, seq):
    L = SSD_CHUNK
    nc = seq // L
    conv_dim = cw.shape[1]
    row = lambda b, c: b * nc + c
    xbc_blk = (PROJ_COLS - conv_dim) // conv_dim
    z_blk = (PROJ_COLS - conv_dim - SSD_WIDTH) // SSD_WIDTH
    dt_blk = 1
    const = lambda b, c: (0, 0)
    return pl.pallas_call(
        _ssd_kernel,
        grid=(batch, nc),
        in_specs=[
            pl.BlockSpec((L, conv_dim), lambda b, c: (row(b, c), xbc_blk)),
            pl.BlockSpec((L, SSD_WIDTH), lambda b, c: (row(b, c), z_blk)),
            pl.BlockSpec((L, LANES), lambda b, c: (row(b, c), dt_blk)),
            pl.BlockSpec((SSD_CONV, conv_dim), const),
            pl.BlockSpec((1, conv_dim), const),
            pl.BlockSpec((1, LANES), const),
            pl.BlockSpec((1, LANES), const),
            pl.BlockSpec((1, SSD_WIDTH), const),
            pl.BlockSpec((1, SSD_WIDTH), const),
        ],
        out_specs=pl.BlockSpec((L, SSD_WIDTH), lambda b, c: (row(b, c), 0)),
        out_shape=jax.ShapeDtypeStruct((batch * seq, SSD_WIDTH), BF16),
        scratch_shapes=[
            pltpu.VMEM((CONV_HALO + L, conv_dim), F32),
            pltpu.VMEM((SSD_GROUPS, SSD_STATE, GROUP_WIDTH), F32),
        ],
        compiler_params=_params(("parallel", "arbitrary")),
        name="ssd",
    )(proj, proj, small, cw, cb, dtb, alog, dskip, nw)


def _outproj_kernel(ya_ref, ys_ref, wa_ref, ws_ref, x_ref, o_ref):
    o_ref[...] = (x_ref[...]
                  + jnp.dot(ya_ref[...], wa_ref[...], preferred_element_type=F32)
                  + jnp.dot(ys_ref[...], ws_ref[...], preferred_element_type=F32))


def _outproj(ya, ys, w, x, layer, *, tm=1024, tn=512):
    m, d = x.shape
    ka, ks = ya.shape[1], ys.shape[1]
    assert ka == ks
    tm, tn = min(tm, m), min(tn, d)
    return pl.pallas_call(
        _outproj_kernel,
        grid=(m // tm, d // tn),
        in_specs=[
            pl.BlockSpec((tm, ka), lambda i, j: (i, 0)),
            pl.BlockSpec((tm, ks), lambda i, j: (i, 0)),
            pl.BlockSpec((None, ka, tn), lambda i, j: (layer, 0, j)),
            pl.BlockSpec((None, ks, tn), lambda i, j: (layer, 1, j)),
            pl.BlockSpec((tm, tn), lambda i, j: (i, j)),
        ],
        out_specs=pl.BlockSpec((tm, tn), lambda i, j: (i, j)),
        out_shape=jax.ShapeDtypeStruct((m, d), F32),
        compiler_params=_params(("parallel", "arbitrary")),
        name="outproj",
    )(ya, ys, w, w, x)


def _rotate_half_cols(w):
    half = w.shape[-1] // 2
    return jnp.concatenate([-w[..., half:], w[..., :half]], axis=-1)


W_IN_TILE = 256
XPOSE_ROWS = 512


def _relayout_w_in_kernel(wt_ref, small_t_ref, o_ref):
    o1 = Q_LORA
    o2 = o1 + KV_LORA
    o3 = o2 + QK_ROPE
    o4 = o3 + SSD_WIDTH
    z_start = SMALL_START + SMALL_COLS

    def put(dst, src, n):
        for r in range(0, n, XPOSE_ROWS):
            rows = min(XPOSE_ROWS, n - r)
            o_ref[:, dst + r:dst + r + rows] = wt_ref[src + r:src + r + rows, :].T.astype(BF16)

    put(0, 0, o1)
    o_ref[:, o1:Q_LORA_PAD] = jnp.zeros((o_ref.shape[0], Q_LORA_PAD - o1), BF16)
    put(Q_LORA_PAD, o1, KV_LORA)
    o_ref[:, SMALL_START:z_start] = small_t_ref[...].T.astype(BF16)
    put(z_start, o3, SSD_WIDTH)
    put(z_start + SSD_WIDTH, o4, o_ref.shape[1] - z_start - SSD_WIDTH)


def _prep_w_in(w_in):
    depth, d, cols = w_in.shape
    wt = jnp.swapaxes(w_in, 1, 2)
    o2 = Q_LORA + KV_LORA
    o3 = o2 + QK_ROPE
    kpe = wt[:, o2:o3]
    half = QK_ROPE // 2
    small_t = jnp.concatenate(
        [kpe, -kpe[:, half:], kpe[:, :half], wt[:, cols - SSD_HEADS:],
         jnp.zeros((depth, SMALL_COLS - 2 * QK_ROPE - SSD_HEADS, d), w_in.dtype)], axis=1)
    tk = min(W_IN_TILE, d)
    return pl.pallas_call(
        _relayout_w_in_kernel,
        grid=(depth, d // tk),
        in_specs=[
            pl.BlockSpec((None, cols, tk), lambda l, r: (l, 0, r)),
            pl.BlockSpec((None, SMALL_COLS, tk), lambda l, r: (l, 0, r)),
        ],
        out_specs=pl.BlockSpec((None, tk, PROJ_COLS), lambda l, r: (l, r, 0)),
        out_shape=jax.ShapeDtypeStruct((depth, d, PROJ_COLS), BF16),
        compiler_params=_params(("parallel", "parallel")),
        name="relayout_w_in",
    )(wt, small_t)


def _prep_w_q(w):
    lead = w.shape[:-1]
    w = w.reshape(lead + (MLA_HEADS, QK_NOPE + QK_ROPE))
    rope = w[..., QK_NOPE:]
    rope = jnp.concatenate([rope, _rotate_half_cols(rope)], axis=-1)
    return jnp.concatenate([w[..., :QK_NOPE].reshape(lead + (-1,)), rope.reshape(lead + (-1,))],
                           axis=-1).astype(BF16)


def _pad_lanes(v):
    return jnp.pad(v, (0, LANES - v.shape[0])).reshape(1, LANES)


def kernel(x, positions, ffn1_norm, ffn1_w_gate, ffn1_w_up, ffn1_w_down, mix_norm, w_in, q_a_norm, w_q_up, kv_a_norm, w_kv_up, conv_w, conv_b, dt_bias, a_log, d_skip, ssd_norm, w_out, ffn2_norm, ffn2_w_gate, ffn2_w_up, ffn2_w_down, final_norm):
    batch, seq, d = x.shape
    depth = w_in.shape[0]
    m = batch * seq
    xs = x.reshape(m, d)
    pos = positions.reshape(m, 1)
    inv_freq = ROPE_THETA ** (-jnp.arange(0, QK_ROPE, 2, dtype=F32) / QK_ROPE)
    freq = jnp.tile(inv_freq, LANES // inv_freq.shape[0]).reshape(1, LANES)

    w1g, w1u, w1d = ffn1_w_gate.astype(BF16), ffn1_w_up.astype(BF16), ffn1_w_down.astype(BF16)
    w2g, w2u, w2d = ffn2_w_gate.astype(BF16), ffn2_w_up.astype(BF16), ffn2_w_down.astype(BF16)
    w_in_b, w_q_b, w_kv_b = _prep_w_in(w_in), _prep_w_q(w_q_up), w_kv_up.astype(BF16)
    w_out_b = w_out.astype(BF16)

    for l in range(depth):
        xs = _ffn(xs, ffn1_norm[l], w1g, w1u, w1d, l)
        proj, small = _inproj(xs, mix_norm[l], w_in_b, l)
        qt, k, vt = _qkv(proj, small, pos, freq, q_a_norm[l], kv_a_norm[l], w_q_b, w_kv_b, l,
                        batch=batch, seq=seq)
        y_attn = _attention(qt, k, vt).reshape(m, MLA_WIDTH)
        y_ssd = _ssd(proj, small, conv_w[l], conv_b[l].reshape(1, -1), _pad_lanes(dt_bias[l]),
                     _pad_lanes(a_log[l]), jnp.repeat(d_skip[l], SSD_HEAD_DIM).reshape(1, -1),
                     ssd_norm[l].reshape(1, -1), batch=batch, seq=seq)
        xs = _outproj(y_attn, y_ssd, w_out_b, xs, l)
        xs = _ffn(xs, ffn2_norm[l], w2g, w2u, w2d, l, final_norm if l == depth - 1 else None)
    return xs.reshape(batch, seq, d)
```

```python
import functools
import math

import jax
import jax.numpy as jnp
from jax import lax
from jax.experimental import pallas as pl
from jax.experimental.pallas import tpu as pltpu

F32 = jnp.float32
BF16 = jnp.bfloat16

MLA_HEADS = 16
Q_LORA = 896
KV_LORA = 512
QK_NOPE = 128
QK_ROPE = 64
V_HEAD = 128
ROPE_THETA = 10000.0
SSD_HEAD_DIM = 64
SSD_HEADS = 32
SSD_GROUPS = 8
SSD_STATE = 128
SSD_CONV = 4
SSD_CHUNK = 128
SSD_WIDTH = SSD_HEADS * SSD_HEAD_DIM
MLA_WIDTH = MLA_HEADS * V_HEAD
EPS = 1e-6

LANES = 128
Q_LORA_PAD = 1024
SMALL_COLS = 512
QK_DIM = 256
PROJ_COLS = Q_LORA_PAD + KV_LORA + SMALL_COLS + SSD_WIDTH + (SSD_WIDTH + 2 * SSD_GROUPS * SSD_STATE)

ATTN_TILE = 512
Q_SCALE = (QK_NOPE + QK_ROPE) ** -0.5 * math.log2(math.e)

DOWN_CHUNK = 1024

NORM_ROWS = 128

VMEM_LIMIT = 60 * 1024 * 1024


def _params(sem):
    return pltpu.CompilerParams(dimension_semantics=sem, vmem_limit_bytes=VMEM_LIMIT)


def _rms(x, g):
    return x * lax.rsqrt(jnp.mean(x * x, axis=-1, keepdims=True) + EPS) * g


def _row_chunks(rows):
    step = min(rows, NORM_ROWS)
    return [slice(r, r + step) for r in range(0, rows, step)]


def _silu(x):
    return x * jax.nn.sigmoid(x)


def _ffn_kernel(x_hbm, g_ref, wg_ref, wu_ref, wd_ref, *rest, final):
    if final:
        fn_ref, o_ref, h_ref, x_ref, x_sem = rest
    else:
        o_ref, h_ref, x_ref, x_sem = rest
    i = pl.program_id(0)
    f = pl.program_id(1)
    tm = x_ref.shape[0]

    def x_copy(tile):
        rows = pl.ds(pl.multiple_of(tile * tm, tm), tm)
        return pltpu.make_async_copy(x_hbm.at[rows, :], x_ref, x_sem)

    @pl.when(f == 0)
    def _():
        @pl.when(i == 0)
        def _():
            x_copy(0).start()

        x_copy(i).wait()
        for rows in _row_chunks(tm):
            x = x_ref[rows, :]
            h_ref[rows, :] = _rms(x, g_ref[...]).astype(BF16)
            o_ref[rows, :] = x

    @pl.when((f == 1) & (i + 1 < pl.num_programs(0)))
    def _():
        x_copy(i + 1).start()

    h = h_ref[...]
    g = jnp.dot(h, wg_ref[...], preferred_element_type=F32)
    u = jnp.dot(h, wu_ref[...], preferred_element_type=F32)
    a = (0.5 * _silu(g) * u).astype(BF16)
    for c0 in range(0, o_ref.shape[1], DOWN_CHUNK):
        cols = slice(c0, c0 + DOWN_CHUNK)
        o_ref[:, cols] += jnp.dot(a, wd_ref[:, cols], preferred_element_type=F32)

    if final:
        @pl.when(f == pl.num_programs(1) - 1)
        def _():
            for rows in _row_chunks(o_ref.shape[0]):
                o_ref[rows, :] = _rms(o_ref[rows, :], fn_ref[...])


def _ffn(x, g, wg, wu, wd, layer, final_norm=None, *, tm=512, tf=512):
    m, d = x.shape
    ff = wg.shape[2]
    tm, tf = min(tm, m), min(tf, ff)
    final = final_norm is not None
    in_specs = [
        pl.BlockSpec(memory_space=pl.ANY),
        pl.BlockSpec((1, d), lambda i, f: (0, 0)),
        pl.BlockSpec((None, d, tf), lambda i, f: (layer, 0, f)),
        pl.BlockSpec((None, d, tf), lambda i, f: (layer, 0, f)),
        pl.BlockSpec((None, tf, d), lambda i, f: (layer, f, 0)),
    ]
    args = [x, g.reshape(1, d), wg, wu, wd]
    if final:
        in_specs.append(pl.BlockSpec((1, d), lambda i, f: (0, 0)))
        args.append(final_norm.reshape(1, d))
    return pl.pallas_call(
        functools.partial(_ffn_kernel, final=final),
        grid=(m // tm, ff // tf),
        in_specs=in_specs,
        out_specs=pl.BlockSpec((tm, d), lambda i, f: (i, 0)),
        out_shape=jax.ShapeDtypeStruct((m, d), F32),
        scratch_shapes=[
            pltpu.VMEM((tm, d), BF16),
            pltpu.VMEM((tm, d), F32),
            pltpu.SemaphoreType.DMA(()),
        ],
        compiler_params=_params(("arbitrary", "arbitrary")),
        name="ffn",
    )(*args)


PROJ_TILE = 1024
SMALL_START = Q_LORA_PAD + KV_LORA


def _inproj_kernel(x_hbm, g_ref, w_ref, o_ref, small_ref, h_ref, x_ref, x_sem):
    i = pl.program_id(0)
    j = pl.program_id(1)
    tm = x_ref.shape[0]

    def x_copy(tile):
        rows = pl.ds(pl.multiple_of(tile * tm, tm), tm)
        return pltpu.make_async_copy(x_hbm.at[rows, :], x_ref, x_sem)

    @pl.when(j == 0)
    def _():
        @pl.when(i == 0)
        def _():
            x_copy(0).start()

        x_copy(i).wait()
        for rows in _row_chunks(tm):
            h_ref[rows, :] = _rms(x_ref[rows, :], g_ref[...]).astype(BF16)

    @pl.when((j == 1) & (i + 1 < pl.num_programs(0)))
    def _():
        x_copy(i + 1).start()

    y = jnp.dot(h_ref[...], w_ref[...], preferred_element_type=F32)
    o_ref[...] = y.astype(BF16)

    @pl.when(j == SMALL_START // PROJ_TILE)
    def _():
        lo = SMALL_START % PROJ_TILE
        small_ref[...] = y[:, lo:lo + SMALL_COLS]


def _inproj(x, g, w, layer, *, tm=1024):
    m, d = x.shape
    n = w.shape[2]
    tm, tn = min(tm, m), PROJ_TILE
    return pl.pallas_call(
        _inproj_kernel,
        grid=(m // tm, n // tn),
        in_specs=[
            pl.BlockSpec(memory_space=pl.ANY),
            pl.BlockSpec((1, d), lambda i, j: (0, 0)),
            pl.BlockSpec((None, d, tn), lambda i, j: (layer, 0, j)),
        ],
        out_specs=[
            pl.BlockSpec((tm, tn), lambda i, j: (i, j)),
            pl.BlockSpec((tm, SMALL_COLS), lambda i, j: (i, 0)),
        ],
        out_shape=[
            jax.ShapeDtypeStruct((m, n), BF16),
            jax.ShapeDtypeStruct((m, SMALL_COLS), F32),
        ],
        scratch_shapes=[
            pltpu.VMEM((tm, d), BF16),
            pltpu.VMEM((tm, d), F32),
            pltpu.SemaphoreType.DMA(()),
        ],
        compiler_params=_params(("arbitrary", "arbitrary")),
        name="inproj",
    )(x, g.reshape(1, d), w)


def _rope_fold(t):
    return t + pltpu.roll(t, QK_ROPE, axis=1)


def _qkv_kernel(cq_ref, ckv_ref, kpe_ref, pos_ref, freq_ref, qn_ref, kvn_ref, wq_ref, wkv_ref,
                qt_ref, k_ref, vt_ref):
    hq = _rms(cq_ref[:, :Q_LORA].astype(F32), qn_ref[...]).astype(BF16)
    hkv = _rms(ckv_ref[...].astype(F32), kvn_ref[...]).astype(BF16)
    yq = jnp.dot(hq, wq_ref[...], preferred_element_type=F32)
    ykv = jnp.dot(hkv, wkv_ref[...], preferred_element_type=F32)

    ang = pos_ref[...].astype(F32) * freq_ref[...]
    lane = lax.broadcasted_iota(jnp.int32, ang.shape, 1)
    first_half = lane < QK_ROPE
    table = jnp.where(first_half, jnp.cos(ang), jnp.sin(ang))

    k_rope = jnp.where(first_half, _rope_fold(kpe_ref[...] * table), 0.0).astype(BF16)
    yq = yq * Q_SCALE
    nope_w = MLA_HEADS * QK_NOPE
    for h in range(MLA_HEADS):
        lo = h * LANES
        qt_ref[0, h, 0, :QK_NOPE, :] = yq[:, lo:lo + QK_NOPE].T.astype(BF16)
        qt_ref[0, h, 0, QK_NOPE:, :] = _rope_fold(yq[:, nope_w + lo:nope_w + lo + LANES] * table).T.astype(BF16)
        kv_lo = h * (QK_NOPE + V_HEAD)
        k_ref[0, h, :, :QK_NOPE] = ykv[:, kv_lo:kv_lo + QK_NOPE].astype(BF16)
        k_ref[0, h, :, QK_NOPE:] = k_rope
        vt_ref[0, h, 0] = ykv[:, kv_lo + QK_NOPE:kv_lo + QK_NOPE + V_HEAD].T.astype(BF16)


def _qkv(proj, small, pos, freq, qn, kvn, wq, wkv, layer, *, batch, seq):
    tm = min(ATTN_TILE, seq)
    nt = seq // tm
    row = lambda b, s: b * nt + s
    kv_blk = Q_LORA_PAD // KV_LORA
    hd = MLA_HEADS
    return pl.pallas_call(
        _qkv_kernel,
        grid=(batch, nt),
        in_specs=[
            pl.BlockSpec((tm, Q_LORA_PAD), lambda b, s: (row(b, s), 0)),
            pl.BlockSpec((tm, KV_LORA), lambda b, s: (row(b, s), kv_blk)),
            pl.BlockSpec((tm, LANES), lambda b, s: (row(b, s), 0)),
            pl.BlockSpec((tm, 1), lambda b, s: (row(b, s), 0)),
            pl.BlockSpec((1, LANES), lambda b, s: (0, 0)),
            pl.BlockSpec((1, Q_LORA), lambda b, s: (0, 0)),
            pl.BlockSpec((1, KV_LORA), lambda b, s: (0, 0)),
            pl.BlockSpec((None,) + wq.shape[1:], lambda b, s: (layer, 0, 0)),
            pl.BlockSpec((None,) + wkv.shape[1:], lambda b, s: (layer, 0, 0)),
        ],
        out_specs=[
            pl.BlockSpec((1, hd, 1, QK_DIM, tm), lambda b, s: (b, 0, s, 0, 0)),
            pl.BlockSpec((1, hd, tm, QK_DIM), lambda b, s: (b, 0, s, 0)),
            pl.BlockSpec((1, hd, 1, V_HEAD, tm), lambda b, s: (b, 0, s, 0, 0)),
        ],
        out_shape=[
            jax.ShapeDtypeStruct((batch, hd, nt, QK_DIM, tm), BF16),
            jax.ShapeDtypeStruct((batch, hd, seq, QK_DIM), BF16),
            jax.ShapeDtypeStruct((batch, hd, nt, V_HEAD, tm), BF16),
        ],
        compiler_params=_params(("parallel", "parallel")),
        name="qkv_up",
    )(proj, proj, small, pos, freq, qn.reshape(1, -1), kvn.reshape(1, -1), wq, wkv)


def _attn_kernel(qt_ref, k_ref, vt_ref, o_ref, s0_ref, s1_ref, m_ref, l_ref, acc_ref, *, nt, tq):
    s_refs = (s0_ref, s1_ref)
    pairs = [(qi, j) for qi in range(nt) for j in range(qi + 1)]

    def scores_into(s_ref, qi, j):
        k = k_ref[0, 0, j * tq:(j + 1) * tq, :]
        s_ref[...] = jnp.dot(k, qt_ref[0, 0, qi], preferred_element_type=F32)

    scores_into(s_refs[0], *pairs[0])
    for t, (qi, j) in enumerate(pairs):
        if t + 1 < len(pairs):
            scores_into(s_refs[(t + 1) % 2], *pairs[t + 1])
        s = s_refs[t % 2][...]
        if j == qi:
            key = lax.broadcasted_iota(jnp.int32, s.shape, 0)
            query = lax.broadcasted_iota(jnp.int32, s.shape, 1)
            s = jnp.where(query >= key, s, -jnp.inf)
        m_tile = jnp.max(s, axis=0, keepdims=True)
        if j == 0:
            m_new = m_tile
        else:
            m_old = m_ref[...]
            m_new = jnp.maximum(m_old, m_tile)
            alpha = jnp.exp2(m_old - m_new)
        p = jnp.exp2(s - m_new)
        l_tile = jnp.sum(p, axis=0, keepdims=True)
        pv = jnp.dot(vt_ref[0, 0, j], p.astype(BF16), preferred_element_type=F32)
        if j == 0:
            l_ref[...] = l_tile
            acc_ref[...] = pv
        else:
            l_ref[...] = alpha * l_ref[...] + l_tile
            acc_ref[...] = alpha * acc_ref[...] + pv
        m_ref[...] = m_new
        if j == qi:
            o_ref[0, qi * tq:(qi + 1) * tq, :] = (acc_ref[...] / l_ref[...]).T.astype(o_ref.dtype)


def _attention(qt, k, vt):
    b, hd, s, _ = k.shape
    nt, tq = vt.shape[2], vt.shape[4]
    return pl.pallas_call(
        functools.partial(_attn_kernel, nt=nt, tq=tq),
        grid=(b, hd),
        in_specs=[
            pl.BlockSpec((1, 1, nt, QK_DIM, tq), lambda bi, h: (bi, h, 0, 0, 0)),
            pl.BlockSpec((1, 1, s, QK_DIM), lambda bi, h: (bi, h, 0, 0)),
            pl.BlockSpec((1, 1, nt, V_HEAD, tq), lambda bi, h: (bi, h, 0, 0, 0)),
        ],
        out_specs=pl.BlockSpec((1, s, V_HEAD), lambda bi, h: (bi, 0, h)),
        out_shape=jax.ShapeDtypeStruct((b, s, hd * V_HEAD), BF16),
        scratch_shapes=[
            pltpu.VMEM((tq, tq), F32),
            pltpu.VMEM((tq, tq), F32),
            pltpu.VMEM((1, tq), F32),
            pltpu.VMEM((1, tq), F32),
            pltpu.VMEM((V_HEAD, tq), F32),
        ],
        compiler_params=_params(("parallel", "parallel")),
        name="mla_attention",
    )(qt, k, vt)


CONV_HALO = 8
HEADS_PER_GROUP = SSD_HEADS // SSD_GROUPS
GROUP_WIDTH = HEADS_PER_GROUP * SSD_HEAD_DIM


def _ssd_kernel(xbc_ref, z_ref, dt_ref, cw_ref, cb_ref, dtb_ref, alog_ref, dskip_ref, nw_ref,
                o_ref, ext_ref, st_ref):
    L = SSD_CHUNK
    c = pl.program_id(1)

    @pl.when(c == 0)
    def _():
        ext_ref[0:CONV_HALO, :] = jnp.zeros((CONV_HALO, ext_ref.shape[1]), F32)
        st_ref[...] = jnp.zeros(st_ref.shape, F32)

    @pl.when(c > 0)
    def _():
        ext_ref[0:CONV_HALO, :] = ext_ref[L:L + CONV_HALO, :]

    ext_ref[CONV_HALO:CONV_HALO + L, :] = xbc_ref[...].astype(F32)
    conv = cb_ref[...]
    for t in range(SSD_CONV):
        off = CONV_HALO - (SSD_CONV - 1) + t
        conv = conv + cw_ref[t:t + 1, :] * ext_ref[off:off + L, :]
    u = _silu(conv)

    stats = _ssd_decays(dt_ref, dtb_ref, alog_ref)
    tril = (lax.broadcasted_iota(jnp.int32, (L, L), 0) >= lax.broadcasted_iota(jnp.int32, (L, L), 1))
    lo_half = lax.broadcasted_iota(jnp.int32, (L, LANES), 1) < SSD_HEAD_DIM
    lo_half_row = lax.broadcasted_iota(jnp.int32, (1, LANES), 1) < SSD_HEAD_DIM
    for g in range(SSD_GROUPS):
        _ssd_group(g, stats, (tril, lo_half, lo_half_row), u, z_ref, dskip_ref, nw_ref, o_ref, st_ref)


def _ssd_decays(dt_ref, dtb_ref, alog_ref):
    L = SSD_CHUNK
    dt = jax.nn.softplus(dt_ref[...] + dtb_ref[...])
    a = dt * (-jnp.exp(alog_ref[...]))
    row = lax.broadcasted_iota(jnp.int32, (L, L), 0)
    col = lax.broadcasted_iota(jnp.int32, (L, L), 1)
    a_cs = jnp.dot((row >= col).astype(F32), a, preferred_element_type=F32,
                   precision=lax.Precision.HIGHEST)
    a_cs_t = jnp.dot(a.T, (row <= col).astype(F32), preferred_element_type=F32,
                     precision=lax.Precision.HIGHEST)
    a_end = a_cs[L - 1:L, :]
    e_cs = jnp.exp(a_cs)
    e_end = jnp.exp(a_end - a_cs)
    return dt, a_cs, a_cs_t, a_end, e_cs, e_end


def _ssd_group(g, stats, masks, u, z_ref, dskip_ref, nw_ref, o_ref, st_ref):
    dt, a_cs, a_cs_t, a_end, e_cs, e_end = stats
    tril, lo_half, lo_half_row = masks
    b_off = SSD_WIDTH
    c_off = SSD_WIDTH + SSD_GROUPS * SSD_STATE

    def pair_cols(v, h0):
        return jnp.where(lo_half, v[:, h0:h0 + 1], v[:, h0 + 1:h0 + 2])

    bg = u[:, b_off + g * SSD_STATE:b_off + (g + 1) * SSD_STATE].astype(BF16)
    cg = u[:, c_off + g * SSD_STATE:c_off + (g + 1) * SSD_STATE].astype(BF16)
    cb = lax.dot_general(cg, bg, (((1,), (1,)), ((), ())), preferred_element_type=F32)
    st = st_ref[g]
    y_off = jnp.dot(cg, st.astype(BF16), preferred_element_type=F32)
    ys, xscs, cds = [], [], []
    for pr in range(HEADS_PER_GROUP // 2):
        h0 = g * HEADS_PER_GROUP + 2 * pr
        x0 = h0 * SSD_HEAD_DIM
        xp = u[:, x0:x0 + LANES]
        xdt = xp * pair_cols(dt, h0)
        ms = []
        for h in (h0, h0 + 1):
            diff = a_cs[:, h:h + 1] - a_cs_t[h:h + 1, :]
            ms.append((cb * jnp.exp(jnp.where(tril, diff, -jnp.inf))).astype(BF16))
        m2 = jnp.concatenate(ms, axis=1)
        xd = jnp.concatenate([jnp.where(lo_half, xdt, 0.0), jnp.where(lo_half, 0.0, xdt)],
                             axis=0).astype(BF16)
        y_diag = jnp.dot(m2, xd, preferred_element_type=F32)
        yo = y_off[:, 2 * pr * SSD_HEAD_DIM:2 * pr * SSD_HEAD_DIM + LANES]
        ys.append(y_diag + pair_cols(e_cs, h0) * yo + dskip_ref[:, x0:x0 + LANES] * xp)
        xscs.append((xdt * pair_cols(e_end, h0)).astype(BF16))
        cds.append(jnp.where(lo_half_row, a_end[:, h0:h0 + 1], a_end[:, h0 + 1:h0 + 2]))
    xsc = jnp.concatenate(xscs, axis=1)
    new = lax.dot_general(bg, xsc, (((0,), (0,)), ((), ())), preferred_element_type=F32)
    st_ref[g] = st * jnp.exp(jnp.concatenate(cds, axis=1)) + new

    gw = slice(g * GROUP_WIDTH, (g + 1) * GROUP_WIDTH)
    yg = jnp.concatenate(ys, axis=1) * _silu(z_ref[:, gw].astype(F32))
    yg = yg * lax.rsqrt(jnp.mean(yg * yg, axis=-1, keepdims=True) + EPS)
    o_ref[:, gw] = (yg * nw_ref[:, gw]).astype(o_ref.dtype)


def _ssd(proj, small, cw, cb, dtb, alog, dskip, nw, *, batch, seq):
    L = SSD_CHUNK
    nc = seq // L
    conv_dim = cw.shape[1]
    row = lambda b, c: b * nc + c
    xbc_blk = (PROJ_COLS - conv_dim) // conv_dim
    z_blk = (PROJ_COLS - conv_dim - SSD_WIDTH) // SSD_WIDTH
    dt_blk = 1
    const = lambda b, c: (0, 0)
    return pl.pallas_call(
        _ssd_kernel,
        grid=(batch, nc),
        in_specs=[
            pl.BlockSpec((L, conv_dim), lambda b, c: (row(b, c), xbc_blk)),
            pl.BlockSpec((L, SSD_WIDTH), lambda b, c: (row(b, c), z_blk)),
            pl.BlockSpec((L, LANES), lambda b, c: (row(b, c), dt_blk)),
            pl.BlockSpec((SSD_CONV, conv_dim), const),
            pl.BlockSpec((1, conv_dim), const),
            pl.BlockSpec((1, LANES), const),
            pl.BlockSpec((1, LANES), const),
            pl.BlockSpec((1, SSD_WIDTH), const),
            pl.BlockSpec((1, SSD_WIDTH), const),
        ],
        out_specs=pl.BlockSpec((L, SSD_WIDTH), lambda b, c: (row(b, c), 0)),
        out_shape=jax.ShapeDtypeStruct((batch * seq, SSD_WIDTH), BF16),
        scratch_shapes=[
            pltpu.VMEM((CONV_HALO + L, conv_dim), F32),
            pltpu.VMEM((SSD_GROUPS, SSD_STATE, GROUP_WIDTH), F32),
        ],
        compiler_params=_params(("parallel", "arbitrary")),
        name="ssd",
    )(proj, proj, small, cw, cb, dtb, alog, dskip, nw)


def _outproj_kernel(ya_hbm, ys_hbm, wa_ref, ws_ref, x_ref, o_ref, ya_buf, ys_buf, sems):
    i = pl.program_id(0)
    j = pl.program_id(1)
    tm = ya_buf.shape[1]
    slot = i % 2

    def copies(tile, into):
        rows = pl.ds(pl.multiple_of(tile * tm, tm), tm)
        return (pltpu.make_async_copy(ya_hbm.at[rows, :], ya_buf.at[into], sems.at[0, into]),
                pltpu.make_async_copy(ys_hbm.at[rows, :], ys_buf.at[into], sems.at[1, into]))

    @pl.when(j == 0)
    def _():
        @pl.when(i == 0)
        def _():
            for c in copies(0, 0):
                c.start()

        for c in copies(i, slot):
            c.wait()

    @pl.when((j == 1) & (i + 1 < pl.num_programs(0)))
    def _():
        for c in copies(i + 1, 1 - slot):
            c.start()

    o_ref[...] = (x_ref[...]
                  + jnp.dot(ya_buf[slot], wa_ref[...], preferred_element_type=F32)
                  + jnp.dot(ys_buf[slot], ws_ref[...], preferred_element_type=F32))


def _outproj(ya, ys, w, x, layer, *, tm=1024, tn=512):
    m, d = x.shape
    ka, ks = ya.shape[1], ys.shape[1]
    assert ka == ks
    tm, tn = min(tm, m), min(tn, d)
    return pl.pallas_call(
        _outproj_kernel,
        grid=(m // tm, d // tn),
        in_specs=[
            pl.BlockSpec(memory_space=pl.ANY),
            pl.BlockSpec(memory_space=pl.ANY),
            pl.BlockSpec((None, ka, tn), lambda i, j: (layer, 0, j)),
            pl.BlockSpec((None, ks, tn), lambda i, j: (layer, 1, j)),
            pl.BlockSpec((tm, tn), lambda i, j: (i, j)),
        ],
        out_specs=pl.BlockSpec((tm, tn), lambda i, j: (i, j)),
        out_shape=jax.ShapeDtypeStruct((m, d), F32),
        scratch_shapes=[
            pltpu.VMEM((2, tm, ka), BF16),
            pltpu.VMEM((2, tm, ks), BF16),
            pltpu.SemaphoreType.DMA((2, 2)),
        ],
        compiler_params=_params(("arbitrary", "arbitrary")),
        name="outproj",
    )(ya, ys, w, w, x)


def _rotate_half_cols(w):
    half = w.shape[-1] // 2
    return jnp.concatenate([-w[..., half:], w[..., :half]], axis=-1)


W_IN_TILE = 256
XPOSE_ROWS = 512


def _relayout_w_in_kernel(wt_ref, small_t_ref, o_ref):
    o1 = Q_LORA
    o2 = o1 + KV_LORA
    o3 = o2 + QK_ROPE
    o4 = o3 + SSD_WIDTH
    z_start = SMALL_START + SMALL_COLS

    def put(dst, src, n):
        for r in range(0, n, XPOSE_ROWS):
            rows = min(XPOSE_ROWS, n - r)
            o_ref[:, dst + r:dst + r + rows] = wt_ref[src + r:src + r + rows, :].T.astype(BF16)

    put(0, 0, o1)
    o_ref[:, o1:Q_LORA_PAD] = jnp.zeros((o_ref.shape[0], Q_LORA_PAD - o1), BF16)
    put(Q_LORA_PAD, o1, KV_LORA)
    o_ref[:, SMALL_START:z_start] = small_t_ref[...].T.astype(BF16)
    put(z_start, o3, SSD_WIDTH)
    put(z_start + SSD_WIDTH, o4, o_ref.shape[1] - z_start - SSD_WIDTH)


def _prep_w_in(w_in):
    depth, d, cols = w_in.shape
    wt = jnp.swapaxes(w_in, 1, 2)
    o2 = Q_LORA + KV_LORA
    o3 = o2 + QK_ROPE
    kpe = wt[:, o2:o3]
    half = QK_ROPE // 2
    small_t = jnp.concatenate(
        [kpe, -kpe[:, half:], kpe[:, :half], wt[:, cols - SSD_HEADS:],
         jnp.zeros((depth, SMALL_COLS - 2 * QK_ROPE - SSD_HEADS, d), w_in.dtype)], axis=1)
    tk = min(W_IN_TILE, d)
    return pl.pallas_call(
        _relayout_w_in_kernel,
        grid=(depth, d // tk),
        in_specs=[
            pl.BlockSpec((None, cols, tk), lambda l, r: (l, 0, r)),
            pl.BlockSpec((None, SMALL_COLS, tk), lambda l, r: (l, 0, r)),
        ],
        out_specs=pl.BlockSpec((None, tk, PROJ_COLS), lambda l, r: (l, r, 0)),
        out_shape=jax.ShapeDtypeStruct((depth, d, PROJ_COLS), BF16),
        compiler_params=_params(("parallel", "parallel")),
        name="relayout_w_in",
    )(wt, small_t)


def _prep_w_q(w):
    lead = w.shape[:-1]
    w = w.reshape(lead + (MLA_HEADS, QK_NOPE + QK_ROPE))
    rope = w[..., QK_NOPE:]
    rope = jnp.concatenate([rope, _rotate_half_cols(rope)], axis=-1)
    return jnp.concatenate([w[..., :QK_NOPE].reshape(lead + (-1,)), rope.reshape(lead + (-1,))],
                           axis=-1).astype(BF16)


def _pad_lanes(v):
    return jnp.pad(v, (0, LANES - v.shape[0])).reshape(1, LANES)


def kernel(x, positions, ffn1_norm, ffn1_w_gate, ffn1_w_up, ffn1_w_down, mix_norm, w_in, q_a_norm, w_q_up, kv_a_norm, w_kv_up, conv_w, conv_b, dt_bias, a_log, d_skip, ssd_norm, w_out, ffn2_norm, ffn2_w_gate, ffn2_w_up, ffn2_w_down, final_norm):
    batch, seq, d = x.shape
    depth = w_in.shape[0]
    m = batch * seq
    xs = x.reshape(m, d)
    pos = positions.reshape(m, 1)
    inv_freq = ROPE_THETA ** (-jnp.arange(0, QK_ROPE, 2, dtype=F32) / QK_ROPE)
    freq = jnp.tile(inv_freq, LANES // inv_freq.shape[0]).reshape(1, LANES)

    w1g, w1u, w1d = ffn1_w_gate.astype(BF16), ffn1_w_up.astype(BF16), ffn1_w_down.astype(BF16)
    w2g, w2u, w2d = ffn2_w_gate.astype(BF16), ffn2_w_up.astype(BF16), ffn2_w_down.astype(BF16)
    w_in_b, w_q_b, w_kv_b = _prep_w_in(w_in), _prep_w_q(w_q_up), w_kv_up.astype(BF16)
    w_out_b = w_out.astype(BF16)

    for l in range(depth):
        xs = _ffn(xs, ffn1_norm[l], w1g, w1u, w1d, l)
        proj, small = _inproj(xs, mix_norm[l], w_in_b, l)
        qt, k, vt = _qkv(proj, small, pos, freq, q_a_norm[l], kv_a_norm[l], w_q_b, w_kv_b, l,
                        batch=batch, seq=seq)
        y_attn = _attention(qt, k, vt).reshape(m, MLA_WIDTH)
        y_ssd = _ssd(proj, small, conv_w[l], conv_b[l].reshape(1, -1), _pad_lanes(dt_bias[l]),
                     _pad_lanes(a_log[l]), jnp.repeat(d_skip[l], SSD_HEAD_DIM).reshape(1, -1),
                     ssd_norm[l].reshape(1, -1), batch=batch, seq=seq)
        xs = _outproj(y_attn, y_ssd, w_out_b, xs, l)
        xs = _ffn(xs, ffn2_norm[l], w2g, w2u, w2d, l, final_norm if l == depth - 1 else None)
    return xs.reshape(batch, seq, d)
```

```python
import functools
import math

import jax
import jax.numpy as jnp
from jax import lax
from jax.experimental import pallas as pl
from jax.experimental.pallas import tpu as pltpu

F32 = jnp.float32
BF16 = jnp.bfloat16

MLA_HEADS = 16
Q_LORA = 896
KV_LORA = 512
QK_NOPE = 128
QK_ROPE = 64
V_HEAD = 128
ROPE_THETA = 10000.0
SSD_HEAD_DIM = 64
SSD_HEADS = 32
SSD_GROUPS = 8
SSD_STATE = 128
SSD_CONV = 4
SSD_CHUNK = 128
SSD_WIDTH = SSD_HEADS * SSD_HEAD_DIM
MLA_WIDTH = MLA_HEADS * V_HEAD
EPS = 1e-6

LANES = 128
Q_LORA_PAD = 1024
SMALL_COLS = 512
QK_DIM = 256
PROJ_COLS = Q_LORA_PAD + KV_LORA + SMALL_COLS + SSD_WIDTH + (SSD_WIDTH + 2 * SSD_GROUPS * SSD_STATE)

ATTN_TILE = 512
Q_SCALE = (QK_NOPE + QK_ROPE) ** -0.5 * math.log2(math.e)

DOWN_CHUNK = 1024

NORM_ROWS = 16

VMEM_LIMIT = 60 * 1024 * 1024


def _params(sem):
    return pltpu.CompilerParams(dimension_semantics=sem, vmem_limit_bytes=VMEM_LIMIT)


def _rms(x, g):
    return x * lax.rsqrt(jnp.mean(x * x, axis=-1, keepdims=True) + EPS) * g


def _row_chunks(rows):
    step = min(rows, NORM_ROWS)
    return [slice(r, r + step) for r in range(0, rows, step)]


def _silu(x):
    return x * jax.nn.sigmoid(x)


def _ffn_kernel(x_hbm, g_ref, wg_ref, wu_ref, wd_ref, *rest, final):
    if final:
        fn_ref, o_ref, h_ref, x_ref, x_sem = rest
    else:
        o_ref, h_ref, x_ref, x_sem = rest
    i = pl.program_id(0)
    f = pl.program_id(1)
    tm = x_ref.shape[0]

    def x_copy(tile):
        rows = pl.ds(pl.multiple_of(tile * tm, tm), tm)
        return pltpu.make_async_copy(x_hbm.at[rows, :], x_ref, x_sem)

    @pl.when(f == 0)
    def _():
        @pl.when(i == 0)
        def _():
            x_copy(0).start()

        x_copy(i).wait()
        for rows in _row_chunks(tm):
            x = x_ref[rows, :]
            h_ref[rows, :] = _rms(x, g_ref[...]).astype(BF16)
            o_ref[rows, :] = x

    @pl.when((f == 1) & (i + 1 < pl.num_programs(0)))
    def _():
        x_copy(i + 1).start()

    h = h_ref[...]
    g = jnp.dot(h, wg_ref[...], preferred_element_type=F32)
    u = jnp.dot(h, wu_ref[...], preferred_element_type=F32)
    a = (0.5 * _silu(g) * u).astype(BF16)
    for c0 in range(0, o_ref.shape[1], DOWN_CHUNK):
        cols = slice(c0, c0 + DOWN_CHUNK)
        o_ref[:, cols] += jnp.dot(a, wd_ref[:, cols], preferred_element_type=F32)

    if final:
        @pl.when(f == pl.num_programs(1) - 1)
        def _():
            for rows in _row_chunks(o_ref.shape[0]):
                o_ref[rows, :] = _rms(o_ref[rows, :], fn_ref[...])


def _ffn(x, g, wg, wu, wd, layer, final_norm=None, *, tm=512, tf=512):
    m, d = x.shape
    ff = wg.shape[2]
    tm, tf = min(tm, m), min(tf, ff)
    final = final_norm is not None
    in_specs = [
        pl.BlockSpec(memory_space=pl.ANY),
        pl.BlockSpec((1, d), lambda i, f: (0, 0)),
        pl.BlockSpec((None, d, tf), lambda i, f: (layer, 0, f)),
        pl.BlockSpec((None, d, tf), lambda i, f: (layer, 0, f)),
        pl.BlockSpec((None, tf, d), lambda i, f: (layer, f, 0)),
    ]
    args = [x, g.reshape(1, d), wg, wu, wd]
    if final:
        in_specs.append(pl.BlockSpec((1, d), lambda i, f: (0, 0)))
        args.append(final_norm.reshape(1, d))
    return pl.pallas_call(
        functools.partial(_ffn_kernel, final=final),
        grid=(m // tm, ff // tf),
        in_specs=in_specs,
        out_specs=pl.BlockSpec((tm, d), lambda i, f: (i, 0)),
        out_shape=jax.ShapeDtypeStruct((m, d), F32),
        scratch_shapes=[
            pltpu.VMEM((tm, d), BF16),
            pltpu.VMEM((tm, d), F32),
            pltpu.SemaphoreType.DMA(()),
        ],
        compiler_params=_params(("arbitrary", "arbitrary")),
        name="ffn",
    )(*args)


PROJ_TILE = 1024
SMALL_START = Q_LORA_PAD + KV_LORA


def _inproj_kernel(x_hbm, g_ref, w_ref, o_ref, small_ref, h_ref, x_ref, x_sem):
    i = pl.program_id(0)
    j = pl.program_id(1)
    tm = x_ref.shape[0]

    def x_copy(tile):
        rows = pl.ds(pl.multiple_of(tile * tm, tm), tm)
        return pltpu.make_async_copy(x_hbm.at[rows, :], x_ref, x_sem)

    @pl.when(j == 0)
    def _():
        @pl.when(i == 0)
        def _():
            x_copy(0).start()

        x_copy(i).wait()
        for rows in _row_chunks(tm):
            h_ref[rows, :] = _rms(x_ref[rows, :], g_ref[...]).astype(BF16)

    @pl.when((j == 1) & (i + 1 < pl.num_programs(0)))
    def _():
        x_copy(i + 1).start()

    y = jnp.dot(h_ref[...], w_ref[...], preferred_element_type=F32)
    o_ref[...] = y.astype(BF16)

    @pl.when(j == SMALL_START // PROJ_TILE)
    def _():
        lo = SMALL_START % PROJ_TILE
        small_ref[...] = y[:, lo:lo + SMALL_COLS]


def _inproj(x, g, w, layer, *, tm=1024):
    m, d = x.shape
    n = w.shape[2]
    tm, tn = min(tm, m), PROJ_TILE
    return pl.pallas_call(
        _inproj_kernel,
        grid=(m // tm, n // tn),
        in_specs=[
            pl.BlockSpec(memory_space=pl.ANY),
            pl.BlockSpec((1, d), lambda i, j: (0, 0)),
            pl.BlockSpec((None, d, tn), lambda i, j: (layer, 0, j)),
        ],
        out_specs=[
            pl.BlockSpec((tm, tn), lambda i, j: (i, j)),
            pl.BlockSpec((tm, SMALL_COLS), lambda i, j: (i, 0)),
        ],
        out_shape=[
            jax.ShapeDtypeStruct((m, n), BF16),
            jax.ShapeDtypeStruct((m, SMALL_COLS), F32),
        ],
        scratch_shapes=[
            pltpu.VMEM((tm, d), BF16),
            pltpu.VMEM((tm, d), F32),
            pltpu.SemaphoreType.DMA(()),
        ],
        compiler_params=_params(("arbitrary", "arbitrary")),
        name="inproj",
    )(x, g.reshape(1, d), w)


def _rope_fold(t):
    return t + pltpu.roll(t, QK_ROPE, axis=1)


def _qkv_kernel(cq_ref, ckv_ref, kpe_ref, pos_ref, freq_ref, qn_ref, kvn_ref, wq_ref, wkv_ref,
                qt_ref, k_ref, vt_ref):
    hq = _rms(cq_ref[:, :Q_LORA].astype(F32), qn_ref[...]).astype(BF16)
    hkv = _rms(ckv_ref[...].astype(F32), kvn_ref[...]).astype(BF16)
    yq = jnp.dot(hq, wq_ref[...], preferred_element_type=F32)
    ykv = jnp.dot(hkv, wkv_ref[...], preferred_element_type=F32)

    ang = pos_ref[...].astype(F32) * freq_ref[...]
    lane = lax.broadcasted_iota(jnp.int32, ang.shape, 1)
    first_half = lane < QK_ROPE
    table = jnp.where(first_half, jnp.cos(ang), jnp.sin(ang))

    k_rope = jnp.where(first_half, _rope_fold(kpe_ref[...] * table), 0.0).astype(BF16)
    yq = yq * Q_SCALE
    nope_w = MLA_HEADS * QK_NOPE
    for h in range(MLA_HEADS):
        lo = h * LANES
        qt_ref[0, h, 0, :QK_NOPE, :] = yq[:, lo:lo + QK_NOPE].T.astype(BF16)
        qt_ref[0, h, 0, QK_NOPE:, :] = _rope_fold(yq[:, nope_w + lo:nope_w + lo + LANES] * table).T.astype(BF16)
        kv_lo = h * (QK_NOPE + V_HEAD)
        k_ref[0, h, :, :QK_NOPE] = ykv[:, kv_lo:kv_lo + QK_NOPE].astype(BF16)
        k_ref[0, h, :, QK_NOPE:] = k_rope
        vt_ref[0, h, 0] = ykv[:, kv_lo + QK_NOPE:kv_lo + QK_NOPE + V_HEAD].T.astype(BF16)


def _qkv(proj, small, pos, freq, qn, kvn, wq, wkv, layer, *, batch, seq):
    tm = min(ATTN_TILE, seq)
    nt = seq // tm
    row = lambda b, s: b * nt + s
    kv_blk = Q_LORA_PAD // KV_LORA
    hd = MLA_HEADS
    return pl.pallas_call(
        _qkv_kernel,
        grid=(batch, nt),
        in_specs=[
            pl.BlockSpec((tm, Q_LORA_PAD), lambda b, s: (row(b, s), 0)),
            pl.BlockSpec((tm, KV_LORA), lambda b, s: (row(b, s), kv_blk)),
            pl.BlockSpec((tm, LANES), lambda b, s: (row(b, s), 0)),
            pl.BlockSpec((tm, 1), lambda b, s: (row(b, s), 0)),
            pl.BlockSpec((1, LANES), lambda b, s: (0, 0)),
            pl.BlockSpec((1, Q_LORA), lambda b, s: (0, 0)),
            pl.BlockSpec((1, KV_LORA), lambda b, s: (0, 0)),
            pl.BlockSpec((None,) + wq.shape[1:], lambda b, s: (layer, 0, 0)),
            pl.BlockSpec((None,) + wkv.shape[1:], lambda b, s: (layer, 0, 0)),
        ],
        out_specs=[
            pl.BlockSpec((1, hd, 1, QK_DIM, tm), lambda b, s: (b, 0, s, 0, 0)),
            pl.BlockSpec((1, hd, tm, QK_DIM), lambda b, s: (b, 0, s, 0)),
            pl.BlockSpec((1, hd, 1, V_HEAD, tm), lambda b, s: (b, 0, s, 0, 0)),
        ],
        out_shape=[
            jax.ShapeDtypeStruct((batch, hd, nt, QK_DIM, tm), BF16),
            jax.ShapeDtypeStruct((batch, hd, seq, QK_DIM), BF16),
            jax.ShapeDtypeStruct((batch, hd, nt, V_HEAD, tm), BF16),
        ],
        compiler_params=_params(("parallel", "parallel")),
        name="qkv_up",
    )(proj, proj, small, pos, freq, qn.reshape(1, -1), kvn.reshape(1, -1), wq, wkv)


def _attn_kernel(qt_ref, k_ref, vt_ref, o_ref, s0_ref, s1_ref, m_ref, l_ref, acc_ref, *, nt, tq):
    s_refs = (s0_ref, s1_ref)
    pairs = [(qi, j) for qi in range(nt) for j in range(qi + 1)]

    def scores_into(s_ref, qi, j):
        k = k_ref[0, 0, j * tq:(j + 1) * tq, :]
        s_ref[...] = jnp.dot(k, qt_ref[0, 0, qi], preferred_element_type=F32)

    scores_into(s_refs[0], *pairs[0])
    for t, (qi, j) in enumerate(pairs):
        if t + 1 < len(pairs):
            scores_into(s_refs[(t + 1) % 2], *pairs[t + 1])
        s = s_refs[t % 2][...]
        if j == qi:
            key = lax.broadcasted_iota(jnp.int32, s.shape, 0)
            query = lax.broadcasted_iota(jnp.int32, s.shape, 1)
            s = jnp.where(query >= key, s, -jnp.inf)
        m_tile = jnp.max(s, axis=0, keepdims=True)
        if j == 0:
            m_new = m_tile
        else:
            m_old = m_ref[...]
            m_new = jnp.maximum(m_old, m_tile)
            alpha = jnp.exp2(m_old - m_new)
        p = jnp.exp2(s - m_new)
        l_tile = jnp.sum(p, axis=0, keepdims=True)
        pv = jnp.dot(vt_ref[0, 0, j], p.astype(BF16), preferred_element_type=F32)
        if j == 0:
            l_ref[...] = l_tile
            acc_ref[...] = pv
        else:
            l_ref[...] = alpha * l_ref[...] + l_tile
            acc_ref[...] = alpha * acc_ref[...] + pv
        m_ref[...] = m_new
        if j == qi:
            o_ref[0, qi * tq:(qi + 1) * tq, :] = (acc_ref[...] / l_ref[...]).T.astype(o_ref.dtype)


def _attention(qt, k, vt):
    b, hd, s, _ = k.shape
    nt, tq = vt.shape[2], vt.shape[4]
    return pl.pallas_call(
        functools.partial(_attn_kernel, nt=nt, tq=tq),
        grid=(b, hd),
        in_specs=[
            pl.BlockSpec((1, 1, nt, QK_DIM, tq), lambda bi, h: (bi, h, 0, 0, 0)),
            pl.BlockSpec((1, 1, s, QK_DIM), lambda bi, h: (bi, h, 0, 0)),
            pl.BlockSpec((1, 1, nt, V_HEAD, tq), lambda bi, h: (bi, h, 0, 0, 0)),
        ],
        out_specs=pl.BlockSpec((1, s, V_HEAD), lambda bi, h: (bi, 0, h)),
        out_shape=jax.ShapeDtypeStruct((b, s, hd * V_HEAD), BF16),
        scratch_shapes=[
            pltpu.VMEM((tq, tq), F32),
            pltpu.VMEM((tq, tq), F32),
            pltpu.VMEM((1, tq), F32),
            pltpu.VMEM((1, tq), F32),
            pltpu.VMEM((V_HEAD, tq), F32),
        ],
        compiler_params=_params(("parallel", "parallel")),
        name="mla_attention",
    )(qt, k, vt)


CONV_HALO = 8
HEADS_PER_GROUP = SSD_HEADS // SSD_GROUPS
GROUP_WIDTH = HEADS_PER_GROUP * SSD_HEAD_DIM


def _ssd_kernel(xbc_ref, z_ref, dt_ref, cw_ref, cb_ref, dtb_ref, alog_ref, dskip_ref, nw_ref,
                o_ref, ext_ref, st_ref):
    L = SSD_CHUNK
    c = pl.program_id(1)

    @pl.when(c == 0)
    def _():
        ext_ref[0:CONV_HALO, :] = jnp.zeros((CONV_HALO, ext_ref.shape[1]), F32)
        st_ref[...] = jnp.zeros(st_ref.shape, F32)

    @pl.when(c > 0)
    def _():
        ext_ref[0:CONV_HALO, :] = ext_ref[L:L + CONV_HALO, :]

    ext_ref[CONV_HALO:CONV_HALO + L, :] = xbc_ref[...].astype(F32)
    conv = cb_ref[...]
    for t in range(SSD_CONV):
        off = CONV_HALO - (SSD_CONV - 1) + t
        conv = conv + cw_ref[t:t + 1, :] * ext_ref[off:off + L, :]
    u = _silu(conv)

    stats = _ssd_decays(dt_ref, dtb_ref, alog_ref)
    tril = (lax.broadcasted_iota(jnp.int32, (L, L), 0) >= lax.broadcasted_iota(jnp.int32, (L, L), 1))
    lo_half = lax.broadcasted_iota(jnp.int32, (L, LANES), 1) < SSD_HEAD_DIM
    lo_half_row = lax.broadcasted_iota(jnp.int32, (1, LANES), 1) < SSD_HEAD_DIM
    for g in range(SSD_GROUPS):
        _ssd_group(g, stats, (tril, lo_half, lo_half_row), u, z_ref, dskip_ref, nw_ref, o_ref, st_ref)


def _ssd_decays(dt_ref, dtb_ref, alog_ref):
    L = SSD_CHUNK
    dt = jax.nn.softplus(dt_ref[...] + dtb_ref[...])
    a = dt * (-jnp.exp(alog_ref[...]))
    row = lax.broadcasted_iota(jnp.int32, (L, L), 0)
    col = lax.broadcasted_iota(jnp.int32, (L, L), 1)
    a_cs = jnp.dot((row >= col).astype(F32), a, preferred_element_type=F32,
                   precision=lax.Precision.HIGHEST)
    a_cs_t = jnp.dot(a.T, (row <= col).astype(F32), preferred_element_type=F32,
                     precision=lax.Precision.HIGHEST)
    a_end = a_cs[L - 1:L, :]
    e_cs = jnp.exp(a_cs)
    e_end = jnp.exp(a_end - a_cs)
    return dt, a_cs, a_cs_t, a_end, e_cs, e_end


def _ssd_group(g, stats, masks, u, z_ref, dskip_ref, nw_ref, o_ref, st_ref):
    dt, a_cs, a_cs_t, a_end, e_cs, e_end = stats
    tril, lo_half, lo_half_row = masks
    b_off = SSD_WIDTH
    c_off = SSD_WIDTH + SSD_GROUPS * SSD_STATE

    def pair_cols(v, h0):
        return jnp.where(lo_half, v[:, h0:h0 + 1], v[:, h0 + 1:h0 + 2])

    bg = u[:, b_off + g * SSD_STATE:b_off + (g + 1) * SSD_STATE].astype(BF16)
    cg = u[:, c_off + g * SSD_STATE:c_off + (g + 1) * SSD_STATE].astype(BF16)
    cb = lax.dot_general(cg, bg, (((1,), (1,)), ((), ())), preferred_element_type=F32)
    st = st_ref[g]
    y_off = jnp.dot(cg, st.astype(BF16), preferred_element_type=F32)
    ys, xscs, cds = [], [], []
    for pr in range(HEADS_PER_GROUP // 2):
        h0 = g * HEADS_PER_GROUP + 2 * pr
        x0 = h0 * SSD_HEAD_DIM
        xp = u[:, x0:x0 + LANES]
        xdt = xp * pair_cols(dt, h0)
        ms = []
        for h in (h0, h0 + 1):
            diff = a_cs[:, h:h + 1] - a_cs_t[h:h + 1, :]
            ms.append((cb * jnp.exp(jnp.where(tril, diff, -jnp.inf))).astype(BF16))
        m2 = jnp.concatenate(ms, axis=1)
        xd = jnp.concatenate([jnp.where(lo_half, xdt, 0.0), jnp.where(lo_half, 0.0, xdt)],
                             axis=0).astype(BF16)
        y_diag = jnp.dot(m2, xd, preferred_element_type=F32)
        yo = y_off[:, 2 * pr * SSD_HEAD_DIM:2 * pr * SSD_HEAD_DIM + LANES]
        ys.append(y_diag + pair_cols(e_cs, h0) * yo + dskip_ref[:, x0:x0 + LANES] * xp)
        xscs.append((xdt * pair_cols(e_end, h0)).astype(BF16))
        cds.append(jnp.where(lo_half_row, a_end[:, h0:h0 + 1], a_end[:, h0 + 1:h0 + 2]))
    xsc = jnp.concatenate(xscs, axis=1)
    new = lax.dot_general(bg, xsc, (((0,), (0,)), ((), ())), preferred_element_type=F32)
    st_ref[g] = st * jnp.exp(jnp.concatenate(cds, axis=1)) + new

    gw = slice(g * GROUP_WIDTH, (g + 1) * GROUP_WIDTH)
    yg = jnp.concatenate(ys, axis=1) * _silu(z_ref[:, gw].astype(F32))
    yg = yg * lax.rsqrt(jnp.mean(yg * yg, axis=-1, keepdims=True) + EPS)
    o_ref[:, gw] = (yg * nw_ref[:, gw]).astype(o_ref.dtype)


def _ssd(proj, small, cw, cb, dtb, alog, dskip, nw, *, batch, seq):
    L = SSD_CHUNK
    nc = seq // L
    conv_dim = cw.shape[1]
    row = lambda b, c: b * nc + c
    xbc_blk = (PROJ_COLS - conv_dim) // conv_dim
    z_blk = (PROJ_COLS - conv_dim - SSD_WIDTH) // SSD_WIDTH
    dt_blk = 1
    const = lambda b, c: (0, 0)
    return pl.pallas_call(
        _ssd_kernel,
        grid=(batch, nc),
        in_specs=[
            pl.BlockSpec((L, conv_dim), lambda b, c: (row(b, c), xbc_blk)),
            pl.BlockSpec((L, SSD_WIDTH), lambda b, c: (row(b, c), z_blk)),
            pl.BlockSpec((L, LANES), lambda b, c: (row(b, c), dt_blk)),
            pl.BlockSpec((SSD_CONV, conv_dim), const),
            pl.BlockSpec((1, conv_dim), const),
            pl.BlockSpec((1, LANES), const),
            pl.BlockSpec((1, LANES), const),
            pl.BlockSpec((1, SSD_WIDTH), const),
            pl.BlockSpec((1, SSD_WIDTH), const),
        ],
        out_specs=pl.BlockSpec((L, SSD_WIDTH), lambda b, c: (row(b, c), 0)),
        out_shape=jax.ShapeDtypeStruct((batch * seq, SSD_WIDTH), BF16),
        scratch_shapes=[
            pltpu.VMEM((CONV_HALO + L, conv_dim), F32),
            pltpu.VMEM((SSD_GROUPS, SSD_STATE, GROUP_WIDTH), F32),
        ],
        compiler_params=_params(("parallel", "arbitrary")),
        name="ssd",
    )(proj, proj, small, cw, cb, dtb, alog, dskip, nw)


def _outproj_kernel(ya_hbm, ys_hbm, wa_ref, ws_ref, x_ref, o_ref, ya_buf, ys_buf, sems):
    i = pl.program_id(0)
    j = pl.program_id(1)
    tm = ya_buf.shape[1]
    slot = i % 2

    def copies(tile, into):
        rows = pl.ds(pl.multiple_of(tile * tm, tm), tm)
        return (pltpu.make_async_copy(ya_hbm.at[rows, :], ya_buf.at[into], sems.at[0, into]),
                pltpu.make_async_copy(ys_hbm.at[rows, :], ys_buf.at[into], sems.at[1, into]))

    @pl.when(j == 0)
    def _():
        @pl.when(i == 0)
        def _():
            for c in copies(0, 0):
                c.start()

        for c in copies(i, slot):
            c.wait()

    @pl.when((j == 1) & (i + 1 < pl.num_programs(0)))
    def _():
        for c in copies(i + 1, 1 - slot):
            c.start()

    o_ref[...] = (x_ref[...]
                  + jnp.dot(ya_buf[slot], wa_ref[...], preferred_element_type=F32)
                  + jnp.dot(ys_buf[slot], ws_ref[...], preferred_element_type=F32))


def _outproj(ya, ys, w, x, layer, *, tm=1024, tn=1024):
    m, d = x.shape
    ka, ks = ya.shape[1], ys.shape[1]
    assert ka == ks
    tm, tn = min(tm, m), min(tn, d)
    return pl.pallas_call(
        _outproj_kernel,
        grid=(m // tm, d // tn),
        in_specs=[
            pl.BlockSpec(memory_space=pl.ANY),
            pl.BlockSpec(memory_space=pl.ANY),
            pl.BlockSpec((None, ka, tn), lambda i, j: (layer, 0, j)),
            pl.BlockSpec((None, ks, tn), lambda i, j: (layer, 1, j)),
            pl.BlockSpec((tm, tn), lambda i, j: (i, j)),
        ],
        out_specs=pl.BlockSpec((tm, tn), lambda i, j: (i, j)),
        out_shape=jax.ShapeDtypeStruct((m, d), F32),
        scratch_shapes=[
            pltpu.VMEM((2, tm, ka), BF16),
            pltpu.VMEM((2, tm, ks), BF16),
            pltpu.SemaphoreType.DMA((2, 2)),
        ],
        compiler_params=_params(("arbitrary", "arbitrary")),
        name="outproj",
    )(ya, ys, w, w, x)


def _rotate_half_cols(w):
    half = w.shape[-1] // 2
    return jnp.concatenate([-w[..., half:], w[..., :half]], axis=-1)


W_IN_TILE = 256
XPOSE_ROWS = 512


def _relayout_w_in_kernel(wt_ref, small_t_ref, o_ref):
    o1 = Q_LORA
    o2 = o1 + KV_LORA
    o3 = o2 + QK_ROPE
    o4 = o3 + SSD_WIDTH
    z_start = SMALL_START + SMALL_COLS

    def put(dst, src, n):
        for r in range(0, n, XPOSE_ROWS):
            rows = min(XPOSE_ROWS, n - r)
            o_ref[:, dst + r:dst + r + rows] = wt_ref[src + r:src + r + rows, :].T.astype(BF16)

    put(0, 0, o1)
    o_ref[:, o1:Q_LORA_PAD] = jnp.zeros((o_ref.shape[0], Q_LORA_PAD - o1), BF16)
    put(Q_LORA_PAD, o1, KV_LORA)
    o_ref[:, SMALL_START:z_start] = small_t_ref[...].T.astype(BF16)
    put(z_start, o3, SSD_WIDTH)
    put(z_start + SSD_WIDTH, o4, o_ref.shape[1] - z_start - SSD_WIDTH)


def _prep_w_in(w_in):
    depth, d, cols = w_in.shape
    wt = jnp.swapaxes(w_in, 1, 2)
    o2 = Q_LORA + KV_LORA
    o3 = o2 + QK_ROPE
    kpe = wt[:, o2:o3]
    half = QK_ROPE // 2
    small_t = jnp.concatenate(
        [kpe, -kpe[:, half:], kpe[:, :half], wt[:, cols - SSD_HEADS:],
         jnp.zeros((depth, SMALL_COLS - 2 * QK_ROPE - SSD_HEADS, d), w_in.dtype)], axis=1)
    tk = min(W_IN_TILE, d)
    return pl.pallas_call(
        _relayout_w_in_kernel,
        grid=(depth, d // tk),
        in_specs=[
            pl.BlockSpec((None, cols, tk), lambda l, r: (l, 0, r)),
            pl.BlockSpec((None, SMALL_COLS, tk), lambda l, r: (l, 0, r)),
        ],
        out_specs=pl.BlockSpec((None, tk, PROJ_COLS), lambda l, r: (l, r, 0)),
        out_shape=jax.ShapeDtypeStruct((depth, d, PROJ_COLS), BF16),
        compiler_params=_params(("parallel", "parallel")),
        name="relayout_w_in",
    )(wt, small_t)


def _prep_w_q(w):
    lead = w.shape[:-1]
    w = w.reshape(lead + (MLA_HEADS, QK_NOPE + QK_ROPE))
    rope = w[..., QK_NOPE:]
    rope = jnp.concatenate([rope, _rotate_half_cols(rope)], axis=-1)
    return jnp.concatenate([w[..., :QK_NOPE].reshape(lead + (-1,)), rope.reshape(lead + (-1,))],
                           axis=-1).astype(BF16)


def _pad_lanes(v):
    return jnp.pad(v, (0, LANES - v.shape[0])).reshape(1, LANES)


def kernel(x, positions, ffn1_norm, ffn1_w_gate, ffn1_w_up, ffn1_w_down, mix_norm, w_in, q_a_norm, w_q_up, kv_a_norm, w_kv_up, conv_w, conv_b, dt_bias, a_log, d_skip, ssd_norm, w_out, ffn2_norm, ffn2_w_gate, ffn2_w_up, ffn2_w_down, final_norm):
    batch, seq, d = x.shape
    depth = w_in.shape[0]
    m = batch * seq
    xs = x.reshape(m, d)
    pos = positions.reshape(m, 1)
    inv_freq = ROPE_THETA ** (-jnp.arange(0, QK_ROPE, 2, dtype=F32) / QK_ROPE)
    freq = jnp.tile(inv_freq, LANES // inv_freq.shape[0]).reshape(1, LANES)

    w1g, w1u, w1d = ffn1_w_gate.astype(BF16), ffn1_w_up.astype(BF16), ffn1_w_down.astype(BF16)
    w2g, w2u, w2d = ffn2_w_gate.astype(BF16), ffn2_w_up.astype(BF16), ffn2_w_down.astype(BF16)
    w_in_b, w_q_b, w_kv_b = _prep_w_in(w_in), _prep_w_q(w_q_up), w_kv_up.astype(BF16)
    w_out_b = w_out.astype(BF16)

    for l in range(depth):
        xs = _ffn(xs, ffn1_norm[l], w1g, w1u, w1d, l)
        proj, small = _inproj(xs, mix_norm[l], w_in_b, l)
        qt, k, vt = _qkv(proj, small, pos, freq, q_a_norm[l], kv_a_norm[l], w_q_b, w_kv_b, l,
                        batch=batch, seq=seq)
        y_attn = _attention(qt, k, vt).reshape(m, MLA_WIDTH)
        y_ssd = _ssd(proj, small, conv_w[l], conv_b[l].reshape(1, -1), _pad_lanes(dt_bias[l]),
                     _pad_lanes(a_log[l]), jnp.repeat(d_skip[l], SSD_HEAD_DIM).reshape(1, -1),
                     ssd_norm[l].reshape(1, -1), batch=batch, seq=seq)
        xs = _outproj(y_attn, y_ssd, w_out_b, xs, l)
        xs = _ffn(xs, ffn2_norm[l], w2g, w2u, w2d, l, final_norm if l == depth - 1 else None)
    return xs.reshape(batch, seq, d)
```
